```python
import jax, jax.numpy as jnp
from jax import lax
import numpy as np

D_MODEL = 1024
BATCH = 8
SEQ = 2048
DEPTH = 1

MIX_WIDTH = D_MODEL
HEAD_DIM = 64
ATTN_WIDTH = MIX_WIDTH // 2
ATTN_HEADS = ATTN_WIDTH // HEAD_DIM
SGU_WIDTH = MIX_WIDTH - ATTN_WIDTH
SGU_GROUP_DIM = 64
SGU_GROUPS = SGU_WIDTH // SGU_GROUP_DIM
SGU_CHUNK = 128
DILATED_PATTERNS = ((128, 1), (512, 4), (2048, 16))
BLOCK_Q = 128
ROPE_THETA = 500000.0
ROT_DIM = HEAD_DIM // 4
ROT_HALF = ROT_DIM // 2
D_FF = ((8 * D_MODEL // 3 + 255) // 256) * 256
IN_PROJ_WIDTH = 3 * ATTN_WIDTH + 2 * SGU_WIDTH
RMS_EPS = 1e-6
LN_EPS = 1e-5

kernel_name = "hymba_dilated_attn_gmlp_sandwich_block"


def rms_norm(x, gain):
    xf = x.astype(jnp.float32)
    y = xf * lax.rsqrt(jnp.mean(xf * xf, axis=-1, keepdims=True) + RMS_EPS)
    return (y * gain.astype(jnp.float32)).astype(x.dtype)


def layer_norm(x, gain, bias):
    xf = x.astype(jnp.float32)
    mu = jnp.mean(xf, axis=-1, keepdims=True)
    xc = xf - mu
    y = xc * lax.rsqrt(jnp.mean(xc * xc, axis=-1, keepdims=True) + LN_EPS)
    return (y * gain.astype(jnp.float32) + bias.astype(jnp.float32)).astype(x.dtype)


def partial_rotary(x, positions):
    inv_freq = ROPE_THETA ** (-jnp.arange(0, ROT_DIM, 2, dtype=jnp.float32) / ROT_DIM)
    ang = positions.astype(jnp.float32)[:, :, None, None] * inv_freq
    cos, sin = jnp.cos(ang), jnp.sin(ang)
    xf = x.astype(jnp.float32)
    x1 = xf[..., :ROT_HALF]
    x2 = xf[..., ROT_HALF:ROT_DIM]
    out = jnp.concatenate([x1 * cos - x2 * sin, x2 * cos + x1 * sin, xf[..., ROT_DIM:]], axis=-1)
    return out.astype(x.dtype)


def dilated_window_attention(q, k, v, window, dilation):
    B, H, S, Dh = q.shape
    L = S // dilation
    W = window // dilation
    nblk = -(-L // BLOCK_Q)
    Lp = nblk * BLOCK_Q

    def strided(t):
        return t.reshape(B, H, L, dilation, Dh).transpose(0, 1, 3, 2, 4)

    qs, ks, vs = strided(q), strided(k), strided(v)
    qs = jnp.pad(qs, ((0, 0), (0, 0), (0, 0), (0, Lp - L), (0, 0)))
    kv_pad = ((0, 0), (0, 0), (0, 0), (W, Lp - L), (0, 0))
    kp = jnp.pad(ks, kv_pad)
    vp = jnp.pad(vs, kv_pad)
    blk = jnp.arange(nblk)[:, None]
    col = jnp.arange(BLOCK_Q + W)[None, :]
    idx = blk * BLOCK_Q + col
    kb = kp[:, :, :, idx]
    vb = vp[:, :, :, idx]
    qb = qs.reshape(B, H, dilation, nblk, BLOCK_Q, Dh)

    scale = 1.0 / np.sqrt(HEAD_DIM)
    s = jnp.einsum('bhrnqd,bhrnkd->bhrnqk', qb, kb,
                   preferred_element_type=jnp.float32) * scale
    row = jnp.arange(BLOCK_Q)[:, None]
    dist = row + W - jnp.arange(BLOCK_Q + W)[None, :]
    key_pos = jnp.arange(nblk)[:, None, None] * BLOCK_Q + jnp.arange(BLOCK_Q + W)[None, None, :] - W
    mask = (dist >= 0)[None] & (dist <= W)[None] & (key_pos >= 0)
    s = jnp.where(mask, s, jnp.finfo(jnp.float32).min)
    m = jnp.max(s, axis=-1, keepdims=True)
    p = jnp.exp(s - m)
    denom = jnp.sum(p, axis=-1, keepdims=True)
    o = jnp.einsum('bhrnqk,bhrnkd->bhrnqd', p, vb.astype(jnp.float32)) / denom
    lse = (m + jnp.log(denom))[..., 0]
    o = o.reshape(B, H, dilation, Lp, Dh)[:, :, :, :L]
    lse = lse.reshape(B, H, dilation, Lp)[:, :, :, :L]
    o = o.transpose(0, 1, 3, 2, 4).reshape(B, H, S, Dh)
    lse = lse.transpose(0, 1, 3, 2).reshape(B, H, S)
    return o, lse


def dilated_mixture_attention(q, k, v):
    outs, lses = [], []
    for window, dilation in DILATED_PATTERNS:
        o, lse = dilated_window_attention(q, k, v, window, dilation)
        outs.append(o)
        lses.append(lse)
    wts = jax.nn.softmax(jnp.stack(lses, axis=0), axis=0)
    o = jnp.sum(wts[..., None] * jnp.stack(outs, axis=0), axis=0)
    return o.astype(q.dtype)


def spatial_gating(u, v, ln_gain, ln_bias, w_spatial, b_spatial):
    B, S, _ = u.shape
    u = jax.nn.gelu(u, approximate=False)
    v = layer_norm(jax.nn.gelu(v, approximate=False), ln_gain, ln_bias)
    vc = v.reshape(B, S // SGU_CHUNK, SGU_CHUNK, SGU_GROUPS, SGU_GROUP_DIM)
    causal = jnp.tril(jnp.ones((SGU_CHUNK, SGU_CHUNK), dtype=bool))
    w = jnp.where(causal[None], w_spatial, 0.0).astype(v.dtype)
    mixed = jnp.einsum('gij,bnjgc->bnigc', w, vc) + b_spatial.T[:, :, None].astype(v.dtype)
    return u * mixed.reshape(B, S, SGU_WIDTH)


def setup_inputs(seed: int = 0) -> dict:
    key = jax.random.key(seed)
    ks = jax.random.split(key, 20)
    f32 = jnp.float32

    def nrm(k, shape, scale):
        return jax.random.normal(k, shape, f32) * scale

    def gain(k, width):
        return 1.0 + 0.05 * jax.random.normal(k, (DEPTH, width), f32)

    x = jax.random.normal(ks[0], (BATCH, SEQ, D_MODEL), f32)
    offsets = jax.random.randint(ks[1], (BATCH, 1), 0, 4096, dtype=jnp.int32)
    positions = (jnp.arange(SEQ, dtype=jnp.int32)[None, :] + offsets).astype(jnp.int32)
    return {
        "x": x,
        "positions": positions,
        "pre_mix_norm": gain(ks[2], D_MODEL),
        "w_in": nrm(ks[3], (DEPTH, D_MODEL, IN_PROJ_WIDTH), D_MODEL ** -0.5),
        "sgu_ln_gain": gain(ks[4], SGU_WIDTH),
        "sgu_ln_bias": nrm(ks[5], (DEPTH, SGU_WIDTH), 0.02),
        "sgu_w_spatial": nrm(ks[6], (DEPTH, SGU_GROUPS, SGU_CHUNK, SGU_CHUNK), 0.5 * SGU_CHUNK ** -0.5),
        "sgu_b_spatial": 1.0 + nrm(ks[7], (DEPTH, SGU_GROUPS, SGU_CHUNK), 0.1),
        "attn_out_norm": gain(ks[8], ATTN_WIDTH),
        "sgu_out_norm": gain(ks[9], SGU_WIDTH),
        "w_out": nrm(ks[10], (DEPTH, MIX_WIDTH, D_MODEL), MIX_WIDTH ** -0.5),
        "post_mix_norm": gain(ks[11], D_MODEL),
        "pre_ffn_norm": gain(ks[12], D_MODEL),
        "w_gate": nrm(ks[13], (DEPTH, D_MODEL, D_FF), D_MODEL ** -0.5),
        "w_up": nrm(ks[14], (DEPTH, D_MODEL, D_FF), D_MODEL ** -0.5),
        "w_down": nrm(ks[15], (DEPTH, D_FF, D_MODEL), D_FF ** -0.5),
        "post_ffn_norm": gain(ks[16], D_MODEL),
    }


def reference(x, positions, pre_mix_norm, w_in, sgu_ln_gain, sgu_ln_bias, sgu_w_spatial,
              sgu_b_spatial, attn_out_norm, sgu_out_norm, w_out, post_mix_norm,
              pre_ffn_norm, w_gate, w_up, w_down, post_ffn_norm):
    B, S, _ = x.shape
    for l in range(DEPTH):
        h = rms_norm(x, pre_mix_norm[l])
        proj = h @ w_in[l]
        q, k, v_a, u, v_s = jnp.split(
            proj, [ATTN_WIDTH, 2 * ATTN_WIDTH, 3 * ATTN_WIDTH, 3 * ATTN_WIDTH + SGU_WIDTH], axis=-1)
        q = partial_rotary(q.reshape(B, S, ATTN_HEADS, HEAD_DIM), positions).transpose(0, 2, 1, 3)
        k = partial_rotary(k.reshape(B, S, ATTN_HEADS, HEAD_DIM), positions).transpose(0, 2, 1, 3)
        v_a = v_a.reshape(B, S, ATTN_HEADS, HEAD_DIM).transpose(0, 2, 1, 3)
        attn = dilated_mixture_attention(q, k, v_a)
        attn = attn.transpose(0, 2, 1, 3).reshape(B, S, ATTN_WIDTH)
        sgu = spatial_gating(u, v_s, sgu_ln_gain[l], sgu_ln_bias[l],
                             sgu_w_spatial[l], sgu_b_spatial[l])
        mixed = jnp.concatenate([rms_norm(attn, attn_out_norm[l]),
                                 rms_norm(sgu, sgu_out_norm[l])], axis=-1)
        y = mixed @ w_out[l]
        x = x + rms_norm(y, post_mix_norm[l])
        h = rms_norm(x, pre_ffn_norm[l])
        f = (jax.nn.silu(h @ w_gate[l]) * (h @ w_up[l])) @ w_down[l]
        x = x + rms_norm(f, post_ffn_norm[l])
    return x
```

```python
import functools

import jax
import jax.numpy as jnp
import numpy as np
from jax import lax
from jax.experimental import pallas as pl
from jax.experimental.pallas import tpu as pltpu

D_MODEL = 1024
HEAD_DIM = 64
ATTN_WIDTH = 512
SGU_WIDTH = 512
SGU_GROUPS = 8
SGU_CHUNK = 128
DILATIONS = (1, 4, 16)
WINDOW_STEPS = 128
BLOCK_Q = 128
ROPE_THETA = 500000.0
ROT_DIM = 16
D_FF = 2816
RMS_EPS = 1e-6
LN_EPS = 1e-5
MASK_VALUE = float(np.finfo(np.float32).min)

LANES = 128
VMEM_LIMIT_BYTES = 56 * 1024 * 1024

TOKEN_TILE = 512


def _rms(x, gain):
    return x * lax.rsqrt(jnp.mean(x * x, axis=-1, keepdims=True) + RMS_EPS) * gain


def _in_proj_kernel(x_ref, pos_ref, gain_ref, invf_ref, w_ref, qkv_ref, uv_ref):
    h = _rms(x_ref[...], gain_ref[...]).astype(jnp.bfloat16)
    proj = jnp.dot(h, w_ref[...], preferred_element_type=jnp.float32)

    ang = pos_ref[...].astype(jnp.float32) * invf_ref[...]
    cos = jnp.cos(ang)
    sin = jnp.sin(ang)
    lane = lax.broadcasted_iota(jnp.int32, ang.shape, 1)
    upper = (lane % ROT_DIM) >= (ROT_DIM // 2)
    sin_from_below = jnp.where(upper, sin, 0.0)
    sin_from_above = jnp.where(upper, 0.0, -sin)

    half = ROT_DIM // 2
    for j in range(2 * ATTN_WIDTH // LANES):
        slab = proj[:, j * LANES:(j + 1) * LANES]
        rot = (slab * cos
               + pltpu.roll(slab, half, axis=1) * sin_from_below
               + pltpu.roll(slab, LANES - half, axis=1) * sin_from_above)
        qkv_ref[:, j * LANES:(j + 1) * LANES] = rot.astype(qkv_ref.dtype)
    qkv_ref[:, 2 * ATTN_WIDTH:] = proj[:, 2 * ATTN_WIDTH:3 * ATTN_WIDTH].astype(qkv_ref.dtype)
    uv_ref[...] = proj[:, 3 * ATTN_WIDTH:].astype(uv_ref.dtype)


def _in_proj(x2, pos2, gain, invf, w_in):
    n_tok = x2.shape[0]
    tm = TOKEN_TILE
    width = w_in.shape[1]
    return pl.pallas_call(
        _in_proj_kernel,
        grid=(n_tok // tm,),
        in_specs=[
            pl.BlockSpec((tm, D_MODEL), lambda i: (i, 0)),
            pl.BlockSpec((tm, 1), lambda i: (i, 0)),
            pl.BlockSpec((1, D_MODEL), lambda i: (0, 0)),
            pl.BlockSpec((1, LANES), lambda i: (0, 0)),
            pl.BlockSpec((D_MODEL, width), lambda i: (0, 0)),
        ],
        out_specs=[
            pl.BlockSpec((tm, 3 * ATTN_WIDTH), lambda i: (i, 0)),
            pl.BlockSpec((tm, 2 * SGU_WIDTH), lambda i: (i, 0)),
        ],
        out_shape=[
            jax.ShapeDtypeStruct((n_tok, 3 * ATTN_WIDTH), jnp.bfloat16),
            jax.ShapeDtypeStruct((n_tok, 2 * SGU_WIDTH), jnp.bfloat16),
        ],
        compiler_params=pltpu.CompilerParams(
            dimension_semantics=("arbitrary",), vmem_limit_bytes=VMEM_LIMIT_BYTES),
        name="in_proj",
    )(x2, pos2, gain, invf, w_in)


def _attn_kernel(q_ref, k_ref, v_ref, o_ref,
                 q32, k32, v32, qp, kp, vp, op, mp, dp, ot, mt, dt, *, seq):
    n_blocks = seq // BLOCK_Q
    f32 = jnp.float32
    bf16 = jnp.bfloat16

    q32[...] = q_ref[...].astype(f32)
    k32[...] = k_ref[...].astype(f32)
    v32[...] = v_ref[...].astype(f32)
    kp[0:WINDOW_STEPS, :] = jnp.zeros((WINDOW_STEPS, LANES), bf16)
    vp[0:WINDOW_STEPS, :] = jnp.zeros((WINDOW_STEPS, LANES), bf16)

    lane = lax.broadcasted_iota(jnp.int32, (BLOCK_Q, LANES), 1)
    first_head = lane < HEAD_DIM

    def head_select(a, b):
        return jnp.where(first_head, a, b)

    for p, dil in enumerate(DILATIONS):
        sub_len = seq // dil
        blocks_per_sub = sub_len // BLOCK_Q
        for r in range(dil):
            rows = pl.ds(r, sub_len, stride=dil) if dil > 1 else pl.ds(0, sub_len)
            qp[r * sub_len:(r + 1) * sub_len, :] = (q32[rows, :] * 0.125).astype(bf16)
            kp[WINDOW_STEPS + r * sub_len:WINDOW_STEPS + (r + 1) * sub_len, :] = k32[rows, :].astype(bf16)
            vp[WINDOW_STEPS + r * sub_len:WINDOW_STEPS + (r + 1) * sub_len, :] = v32[rows, :].astype(bf16)

        with_prev = blocks_per_sub > 1
        n_keys = 2 * BLOCK_Q if with_prev else BLOCK_Q
        row = lax.broadcasted_iota(jnp.int32, (BLOCK_Q, n_keys), 0)
        col = lax.broadcasted_iota(jnp.int32, (BLOCK_Q, n_keys), 1)
        upper_ok = col <= row + (n_keys - BLOCK_Q)

        def block(i, carry):
            off = pl.multiple_of(i * BLOCK_Q, BLOCK_Q)
            qb = qp[pl.ds(off, BLOCK_Q), :]
            if with_prev:
                kw = kp[pl.ds(off, n_keys), :]
                vw = vp[pl.ds(off, n_keys), :]
                has_prev = ((i % blocks_per_sub) != 0).astype(jnp.int32)
                lower = row * has_prev + BLOCK_Q * (1 - has_prev)
                mask = upper_ok & (col >= lower)
            else:
                kw = kp[pl.ds(off + WINDOW_STEPS, n_keys), :]
                vw = vp[pl.ds(off + WINDOW_STEPS, n_keys), :]
                mask = upper_ok
            outs, maxes, dens = [], [], []
            for hd in range(2):
                keep = first_head if hd == 0 else jnp.logical_not(first_head)
                qh = jnp.where(keep, qb, jnp.zeros_like(qb))
                s = lax.dot_general(qh, kw, (((1,), (1,)), ((), ())),
                                    preferred_element_type=f32)
                s = jnp.where(mask, s, MASK_VALUE)
                m = jnp.max(s, axis=-1, keepdims=True)
                e = jnp.exp(s - m)
                dens.append(jnp.sum(e, axis=-1, keepdims=True))
                maxes.append(m)
                outs.append(jnp.dot(e.astype(bf16), vw, preferred_element_type=f32))
            op[pl.ds(off, BLOCK_Q), :] = head_select(outs[0], outs[1])
            mp[pl.ds(off, BLOCK_Q), :] = head_select(
                jnp.broadcast_to(maxes[0], (BLOCK_Q, LANES)), jnp.broadcast_to(maxes[1], (BLOCK_Q, LANES)))
            dp[pl.ds(off, BLOCK_Q), :] = head_select(
                jnp.broadcast_to(dens[0], (BLOCK_Q, LANES)), jnp.broadcast_to(dens[1], (BLOCK_Q, LANES)))
            return carry

        lax.fori_loop(0, n_blocks, block, 0)

        for r in range(dil):
            rows = pl.ds(r, sub_len, stride=dil) if dil > 1 else pl.ds(0, sub_len)
            ot[p, rows, :] = op[r * sub_len:(r + 1) * sub_len, :]
            mt[p, rows, :] = mp[r * sub_len:(r + 1) * sub_len, :]
            dt[p, rows, :] = dp[r * sub_len:(r + 1) * sub_len, :]

    def merge(i, carry):
        rows = pl.ds(pl.multiple_of(i * BLOCK_Q, BLOCK_Q), BLOCK_Q)
        maxes = [mt[p, rows, :] for p in range(len(DILATIONS))]
        m_all = jnp.maximum(jnp.maximum(maxes[0], maxes[1]), maxes[2])
        num = jnp.zeros((BLOCK_Q, LANES), f32)
        den = jnp.zeros((BLOCK_Q, LANES), f32)
        for p in range(len(DILATIONS)):
            w = jnp.exp(maxes[p] - m_all)
            num = num + w * ot[p, rows, :]
            den = den + w * dt[p, rows, :]
        o_ref[rows, :] = (num / den).astype(o_ref.dtype)
        return carry

    lax.fori_loop(0, n_blocks, merge, 0)


def _attention(qkv, batch, seq):
    n_tok = qkv.shape[0]
    n_pairs = ATTN_WIDTH // LANES
    f32, bf16 = jnp.float32, jnp.bfloat16
    n_pat = len(DILATIONS)
    return pl.pallas_call(
        functools.partial(_attn_kernel, seq=seq),
        grid=(batch, n_pairs),
        in_specs=[
            pl.BlockSpec((seq, LANES), lambda b, h: (b, h)),
            pl.BlockSpec((seq, LANES), lambda b, h: (b, n_pairs + h)),
            pl.BlockSpec((seq, LANES), lambda b, h: (b, 2 * n_pairs + h)),
        ],
        out_specs=pl.BlockSpec((seq, LANES), lambda b, h: (b, h)),
        out_shape=jax.ShapeDtypeStruct((n_tok, ATTN_WIDTH), bf16),
        scratch_shapes=[
            pltpu.VMEM((seq, LANES), f32),
            pltpu.VMEM((seq, LANES), f32),
            pltpu.VMEM((seq, LANES), f32),
            pltpu.VMEM((seq, LANES), bf16),
            pltpu.VMEM((seq + WINDOW_STEPS, LANES), bf16),
            pltpu.VMEM((seq + WINDOW_STEPS, LANES), bf16),
            pltpu.VMEM((seq, LANES), f32),
            pltpu.VMEM((seq, LANES), f32),
            pltpu.VMEM((seq, LANES), f32),
            pltpu.VMEM((n_pat, seq, LANES), f32),
            pltpu.VMEM((n_pat, seq, LANES), f32),
            pltpu.VMEM((n_pat, seq, LANES), f32),
        ],
        compiler_params=pltpu.CompilerParams(
            dimension_semantics=("arbitrary", "arbitrary"), vmem_limit_bytes=VMEM_LIMIT_BYTES),
        name="attention",
    )(qkv, qkv, qkv)


def _gelu(x):
    return 0.5 * x * (1.0 + lax.erf(x * np.float32(1.0 / np.sqrt(2.0))))


def _sgu_kernel(u_ref, v_ref, gain_ref, bias_ref, w_ref, b_ref, o_ref):
    f32 = jnp.float32
    u = _gelu(u_ref[...].astype(f32))
    v = _gelu(v_ref[...].astype(f32))
    mu = jnp.mean(v, axis=-1, keepdims=True)
    vc = v - mu
    v = vc * lax.rsqrt(jnp.mean(vc * vc, axis=-1, keepdims=True) + LN_EPS)
    v = (v * gain_ref[...] + bias_ref[...]).astype(jnp.bfloat16)

    row = lax.broadcasted_iota(jnp.int32, (SGU_CHUNK, SGU_CHUNK), 0)
    col = lax.broadcasted_iota(jnp.int32, (SGU_CHUNK, SGU_CHUNK), 1)
    causal = col <= row
    lane = lax.broadcasted_iota(jnp.int32, (SGU_CHUNK, LANES), 1)
    first_group = lane < (SGU_WIDTH // SGU_GROUPS)
    weights = [jnp.where(causal, w_ref[g], 0.0).astype(jnp.bfloat16) for g in range(SGU_GROUPS)]

    n_chunks = u_ref.shape[0] // SGU_CHUNK
    for n in range(n_chunks):
        rows = slice(n * SGU_CHUNK, (n + 1) * SGU_CHUNK)
        for sl in range(SGU_WIDTH // LANES):
            cols = slice(sl * LANES, (sl + 1) * LANES)
            vs = v[rows, cols]
            lo = jnp.dot(weights[2 * sl], vs, preferred_element_type=f32)
            hi = jnp.dot(weights[2 * sl + 1], vs, preferred_element_type=f32)
            mixed = jnp.where(first_group, lo, hi) + b_ref[:, cols]
            o_ref[rows, cols] = (u[rows, cols] * mixed).astype(o_ref.dtype)


def _sgu(uv, ln_gain, ln_bias, w_spatial, b_mat):
    n_tok = uv.shape[0]
    tm = TOKEN_TILE
    n_slabs = SGU_WIDTH // LANES
    return pl.pallas_call(
        _sgu_kernel,
        grid=(n_tok // tm,),
        in_specs=[
            pl.BlockSpec((tm, SGU_WIDTH), lambda i: (i, 0)),
            pl.BlockSpec((tm, SGU_WIDTH), lambda i: (i, 1)),
            pl.BlockSpec((1, SGU_WIDTH), lambda i: (0, 0)),
            pl.BlockSpec((1, SGU_WIDTH), lambda i: (0, 0)),
            pl.BlockSpec((SGU_GROUPS, SGU_CHUNK, SGU_CHUNK), lambda i: (0, 0, 0)),
            pl.BlockSpec((SGU_CHUNK, SGU_WIDTH), lambda i: (0, 0)),
        ],
        out_specs=pl.BlockSpec((tm, SGU_WIDTH), lambda i: (i, 0)),
        out_shape=jax.ShapeDtypeStruct((n_tok, SGU_WIDTH), jnp.bfloat16),
        compiler_params=pltpu.CompilerParams(
            dimension_semantics=("arbitrary",), vmem_limit_bytes=VMEM_LIMIT_BYTES),
        name="sgu",
    )(uv, uv, ln_gain, ln_bias, w_spatial, b_mat)


def _out_ffn_kernel(x_ref, attn_ref, sgu_ref, ga_ref, gs_ref, wo_ref, gpm_ref, gpf_ref,
                    wg_ref, wu_ref, wd_ref, gpo_ref, o_ref):
    f32 = jnp.float32
    bf16 = jnp.bfloat16
    a = _rms(attn_ref[...].astype(f32), ga_ref[...]).astype(bf16)
    s = _rms(sgu_ref[...].astype(f32), gs_ref[...]).astype(bf16)
    y = (jnp.dot(a, wo_ref[0:ATTN_WIDTH, :], preferred_element_type=f32)
         + jnp.dot(s, wo_ref[ATTN_WIDTH:, :], preferred_element_type=f32))
    x1 = x_ref[...] + _rms(y, gpm_ref[...])
    h = _rms(x1, gpf_ref[...]).astype(bf16)
    gate = jnp.dot(h, wg_ref[...], preferred_element_type=f32)
    up = jnp.dot(h, wu_ref[...], preferred_element_type=f32)
    act = (gate * jax.nn.sigmoid(gate) * up).astype(bf16)
    f = jnp.dot(act, wd_ref[...], preferred_element_type=f32)
    o_ref[...] = x1 + _rms(f, gpo_ref[...])


def _out_ffn(x2, attn, sgu, ga, gs, wo, gpm, gpf, wg, wu, wd, gpo):
    n_tok = x2.shape[0]
    tm = 256

    def const(shape):
        return pl.BlockSpec(shape, lambda i: (0,) * len(shape), pipeline_mode=pl.Buffered(1))

    return pl.pallas_call(
        _out_ffn_kernel,
        grid=(n_tok // tm,),
        in_specs=[
            pl.BlockSpec((tm, D_MODEL), lambda i: (i, 0)),
            pl.BlockSpec((tm, ATTN_WIDTH), lambda i: (i, 0)),
            pl.BlockSpec((tm, SGU_WIDTH), lambda i: (i, 0)),
            const((1, ATTN_WIDTH)),
            const((1, SGU_WIDTH)),
            const((D_MODEL, D_MODEL)),
            const((1, D_MODEL)),
            const((1, D_MODEL)),
            const((D_MODEL, D_FF)),
            const((D_MODEL, D_FF)),
            const((D_FF, D_MODEL)),
            const((1, D_MODEL)),
        ],
        out_specs=pl.BlockSpec((tm, D_MODEL), lambda i: (i, 0)),
        out_shape=jax.ShapeDtypeStruct((n_tok, D_MODEL), jnp.float32),
        compiler_params=pltpu.CompilerParams(
            dimension_semantics=("arbitrary",), vmem_limit_bytes=VMEM_LIMIT_BYTES),
        name="out_ffn",
    )(x2, attn, sgu, ga, gs, wo, gpm, gpf, wg, wu, wd, gpo)


def kernel(x, positions, pre_mix_norm, w_in, sgu_ln_gain, sgu_ln_bias, sgu_w_spatial, sgu_b_spatial, attn_out_norm, sgu_out_norm, w_out, post_mix_norm, pre_ffn_norm, w_gate, w_up, w_down, post_ffn_norm):
    batch, seq, _ = x.shape
    bf16 = jnp.bfloat16
    n_tok = batch * seq
    x2 = x.reshape(n_tok, D_MODEL)
    pos2 = positions.reshape(n_tok, 1)

    inv_freq = ROPE_THETA ** (-jnp.arange(0, ROT_DIM, 2, dtype=jnp.float32) / ROT_DIM)
    lane = jnp.arange(LANES)
    in_rot = (lane % HEAD_DIM) < ROT_DIM
    invf = jnp.where(in_rot, inv_freq[lane % (ROT_DIM // 2)], 0.0).reshape(1, LANES)

    depth = w_in.shape[0]
    for l in range(depth):
        qkv, uv = _in_proj(x2, pos2, pre_mix_norm[l][None, :], invf, w_in[l].astype(bf16))
        attn = _attention(qkv, batch, seq)
        b_mat = jnp.repeat(sgu_b_spatial[l].T, SGU_WIDTH // SGU_GROUPS, axis=1)
        sgu = _sgu(uv, sgu_ln_gain[l][None, :], sgu_ln_bias[l][None, :], sgu_w_spatial[l], b_mat)
        x2 = _out_ffn(x2, attn, sgu, attn_out_norm[l][None, :], sgu_out_norm[l][None, :],
                      w_out[l].astype(bf16), post_mix_norm[l][None, :], pre_ffn_norm[l][None, :],
                      w_gate[l].astype(bf16), w_up[l].astype(bf16), w_down[l].astype(bf16),
                      post_ffn_norm[l][None, :])
    return x2.reshape(batch, seq, D_MODEL)
```

```python
import functools

import jax
import jax.numpy as jnp
import numpy as np
from jax import lax
from jax.experimental import pallas as pl
from jax.experimental.pallas import tpu as pltpu

D_MODEL = 1024
HEAD_DIM = 64
ATTN_WIDTH = 512
SGU_WIDTH = 512
SGU_GROUPS = 8
SGU_CHUNK = 128
DILATIONS = (1, 4, 16)
WINDOW_STEPS = 128
BLOCK_Q = 128
BLOCK_UNROLL = 16
ROPE_THETA = 500000.0
ROT_DIM = 16
D_FF = 2816
RMS_EPS = 1e-6
LN_EPS = 1e-5
MASK_VALUE = float(np.finfo(np.float32).min)

LANES = 128
VMEM_LIMIT_BYTES = 56 * 1024 * 1024

TOKEN_TILE = 512


def _rms(x, gain):
    return x * lax.rsqrt(jnp.mean(x * x, axis=-1, keepdims=True) + RMS_EPS) * gain


def _in_proj_kernel(x_ref, pos_ref, gain_ref, invf_ref, w_ref, qkv_ref, uv_ref):
    h = _rms(x_ref[...], gain_ref[...]).astype(jnp.bfloat16)
    proj = jnp.dot(h, w_ref[...], preferred_element_type=jnp.float32)

    ang = pos_ref[...].astype(jnp.float32) * invf_ref[...]
    cos = jnp.cos(ang)
    sin = jnp.sin(ang)
    lane = lax.broadcasted_iota(jnp.int32, ang.shape, 1)
    upper = (lane % ROT_DIM) >= (ROT_DIM // 2)
    sin_from_below = jnp.where(upper, sin, 0.0)
    sin_from_above = jnp.where(upper, 0.0, -sin)

    half = ROT_DIM // 2
    for j in range(2 * ATTN_WIDTH // LANES):
        slab = proj[:, j * LANES:(j + 1) * LANES]
        rot = (slab * cos
               + pltpu.roll(slab, half, axis=1) * sin_from_below
               + pltpu.roll(slab, LANES - half, axis=1) * sin_from_above)
        qkv_ref[:, j * LANES:(j + 1) * LANES] = rot.astype(qkv_ref.dtype)
    qkv_ref[:, 2 * ATTN_WIDTH:] = proj[:, 2 * ATTN_WIDTH:3 * ATTN_WIDTH].astype(qkv_ref.dtype)
    uv_ref[...] = proj[:, 3 * ATTN_WIDTH:].astype(uv_ref.dtype)


def _in_proj(x2, pos2, gain, invf, w_in):
    n_tok = x2.shape[0]
    tm = TOKEN_TILE
    width = w_in.shape[1]
    return pl.pallas_call(
        _in_proj_kernel,
        grid=(n_tok // tm,),
        in_specs=[
            pl.BlockSpec((tm, D_MODEL), lambda i: (i, 0)),
            pl.BlockSpec((tm, 1), lambda i: (i, 0)),
            pl.BlockSpec((1, D_MODEL), lambda i: (0, 0)),
            pl.BlockSpec((1, LANES), lambda i: (0, 0)),
            pl.BlockSpec((D_MODEL, width), lambda i: (0, 0)),
        ],
        out_specs=[
            pl.BlockSpec((tm, 3 * ATTN_WIDTH), lambda i: (i, 0)),
            pl.BlockSpec((tm, 2 * SGU_WIDTH), lambda i: (i, 0)),
        ],
        out_shape=[
            jax.ShapeDtypeStruct((n_tok, 3 * ATTN_WIDTH), jnp.bfloat16),
            jax.ShapeDtypeStruct((n_tok, 2 * SGU_WIDTH), jnp.bfloat16),
        ],
        compiler_params=pltpu.CompilerParams(
            dimension_semantics=("arbitrary",), vmem_limit_bytes=VMEM_LIMIT_BYTES),
        name="in_proj",
    )(x2, pos2, gain, invf, w_in)


def _attn_kernel(q_ref, k_ref, v_ref, o_ref,
                 q32, k32, v32, qp, kp, vp, op, mp, dp, ot, mt, dt, *, seq):
    n_blocks = seq // BLOCK_Q
    f32 = jnp.float32
    bf16 = jnp.bfloat16

    q32[...] = q_ref[...].astype(f32)
    k32[...] = k_ref[...].astype(f32)
    v32[...] = v_ref[...].astype(f32)
    kp[0:WINDOW_STEPS, :] = jnp.zeros((WINDOW_STEPS, LANES), bf16)
    vp[0:WINDOW_STEPS, :] = jnp.zeros((WINDOW_STEPS, LANES), bf16)

    lane = lax.broadcasted_iota(jnp.int32, (BLOCK_Q, LANES), 1)
    first_head = lane < HEAD_DIM

    def head_select(a, b):
        return jnp.where(first_head, a, b)

    for p, dil in enumerate(DILATIONS):
        sub_len = seq // dil
        blocks_per_sub = sub_len // BLOCK_Q
        for r in range(dil):
            rows = pl.ds(r, sub_len, stride=dil) if dil > 1 else pl.ds(0, sub_len)
            qp[r * sub_len:(r + 1) * sub_len, :] = (q32[rows, :] * 0.125).astype(bf16)
            kp[WINDOW_STEPS + r * sub_len:WINDOW_STEPS + (r + 1) * sub_len, :] = k32[rows, :].astype(bf16)
            vp[WINDOW_STEPS + r * sub_len:WINDOW_STEPS + (r + 1) * sub_len, :] = v32[rows, :].astype(bf16)

        with_prev = blocks_per_sub > 1
        n_keys = 2 * BLOCK_Q if with_prev else BLOCK_Q
        row = lax.broadcasted_iota(jnp.int32, (BLOCK_Q, n_keys), 0)
        col = lax.broadcasted_iota(jnp.int32, (BLOCK_Q, n_keys), 1)
        upper_ok = col <= row + (n_keys - BLOCK_Q)

        def block(i):
            off = pl.multiple_of(i * BLOCK_Q, BLOCK_Q)
            qb = qp[pl.ds(off, BLOCK_Q), :]
            if with_prev:
                kw = kp[pl.ds(off, n_keys), :]
                vw = vp[pl.ds(off, n_keys), :]
                has_prev = ((i % blocks_per_sub) != 0).astype(jnp.int32)
                lower = row * has_prev + BLOCK_Q * (1 - has_prev)
                mask = upper_ok & (col >= lower)
            else:
                kw = kp[pl.ds(off + WINDOW_STEPS, n_keys), :]
                vw = vp[pl.ds(off + WINDOW_STEPS, n_keys), :]
                mask = upper_ok
            zero = jnp.zeros_like(qb)
            q2 = jnp.concatenate([jnp.where(first_head, qb, zero), jnp.where(first_head, zero, qb)], axis=0)
            s = lax.dot_general(q2, kw, (((1,), (1,)), ((), ())), preferred_element_type=f32)
            s = jnp.where(jnp.concatenate([mask, mask], axis=0), s, MASK_VALUE)
            m = jnp.max(s, axis=-1, keepdims=True)
            e = jnp.exp(s - m)
            den = jnp.sum(e, axis=-1, keepdims=True)
            o2 = jnp.dot(e.astype(bf16), vw, preferred_element_type=f32)
            op[pl.ds(off, BLOCK_Q), :] = head_select(o2[:BLOCK_Q], o2[BLOCK_Q:])
            mp[pl.ds(off, BLOCK_Q), :] = head_select(
                jnp.broadcast_to(m[:BLOCK_Q], (BLOCK_Q, LANES)), jnp.broadcast_to(m[BLOCK_Q:], (BLOCK_Q, LANES)))
            dp[pl.ds(off, BLOCK_Q), :] = head_select(
                jnp.broadcast_to(den[:BLOCK_Q], (BLOCK_Q, LANES)), jnp.broadcast_to(den[BLOCK_Q:], (BLOCK_Q, LANES)))

        def block_group(g, carry):
            for u in range(BLOCK_UNROLL):
                block(g * BLOCK_UNROLL + u)
            return carry

        lax.fori_loop(0, n_blocks // BLOCK_UNROLL, block_group, 0)

        for r in range(dil):
            rows = pl.ds(r, sub_len, stride=dil) if dil > 1 else pl.ds(0, sub_len)
            ot[p, rows, :] = op[r * sub_len:(r + 1) * sub_len, :]
            mt[p, rows, :] = mp[r * sub_len:(r + 1) * sub_len, :]
            dt[p, rows, :] = dp[r * sub_len:(r + 1) * sub_len, :]

    def merge(i, carry):
        rows = pl.ds(pl.multiple_of(i * BLOCK_Q, BLOCK_Q), BLOCK_Q)
        maxes = [mt[p, rows, :] for p in range(len(DILATIONS))]
        m_all = jnp.maximum(jnp.maximum(maxes[0], maxes[1]), maxes[2])
        num = jnp.zeros((BLOCK_Q, LANES), f32)
        den = jnp.zeros((BLOCK_Q, LANES), f32)
        for p in range(len(DILATIONS)):
            w = jnp.exp(maxes[p] - m_all)
            num = num + w * ot[p, rows, :]
            den = den + w * dt[p, rows, :]
        o_ref[rows, :] = (num / den).astype(o_ref.dtype)
        return carry

    lax.fori_loop(0, n_blocks, merge, 0)


def _attention(qkv, batch, seq):
    n_tok = qkv.shape[0]
    n_pairs = ATTN_WIDTH // LANES
    f32, bf16 = jnp.float32, jnp.bfloat16
    n_pat = len(DILATIONS)
    return pl.pallas_call(
        functools.partial(_attn_kernel, seq=seq),
        grid=(batch, n_pairs),
        in_specs=[
            pl.BlockSpec((seq, LANES), lambda b, h: (b, h)),
            pl.BlockSpec((seq, LANES), lambda b, h: (b, n_pairs + h)),
            pl.BlockSpec((seq, LANES), lambda b, h: (b, 2 * n_pairs + h)),
        ],
        out_specs=pl.BlockSpec((seq, LANES), lambda b, h: (b, h)),
        out_shape=jax.ShapeDtypeStruct((n_tok, ATTN_WIDTH), bf16),
        scratch_shapes=[
            pltpu.VMEM((seq, LANES), f32),
            pltpu.VMEM((seq, LANES), f32),
            pltpu.VMEM((seq, LANES), f32),
            pltpu.VMEM((seq, LANES), bf16),
            pltpu.VMEM((seq + WINDOW_STEPS, LANES), bf16),
            pltpu.VMEM((seq + WINDOW_STEPS, LANES), bf16),
            pltpu.VMEM((seq, LANES), f32),
            pltpu.VMEM((seq, LANES), f32),
            pltpu.VMEM((seq, LANES), f32),
            pltpu.VMEM((n_pat, seq, LANES), f32),
            pltpu.VMEM((n_pat, seq, LANES), f32),
            pltpu.VMEM((n_pat, seq, LANES), f32),
        ],
        compiler_params=pltpu.CompilerParams(
            dimension_semantics=("arbitrary", "arbitrary"), vmem_limit_bytes=VMEM_LIMIT_BYTES),
        name="attention",
    )(qkv, qkv, qkv)


def _gelu(x):
    return 0.5 * x * (1.0 + lax.erf(x * np.float32(1.0 / np.sqrt(2.0))))


def _sgu_kernel(u_ref, v_ref, gain_ref, bias_ref, w_ref, b_ref, o_ref):
    f32 = jnp.float32
    u = _gelu(u_ref[...].astype(f32))
    v = _gelu(v_ref[...].astype(f32))
    mu = jnp.mean(v, axis=-1, keepdims=True)
    vc = v - mu
    v = vc * lax.rsqrt(jnp.mean(vc * vc, axis=-1, keepdims=True) + LN_EPS)
    v = (v * gain_ref[...] + bias_ref[...]).astype(jnp.bfloat16)

    row = lax.broadcasted_iota(jnp.int32, (SGU_CHUNK, SGU_CHUNK), 0)
    col = lax.broadcasted_iota(jnp.int32, (SGU_CHUNK, SGU_CHUNK), 1)
    causal = col <= row
    lane = lax.broadcasted_iota(jnp.int32, (SGU_CHUNK, LANES), 1)
    first_group = lane < (SGU_WIDTH // SGU_GROUPS)
    weights = [jnp.where(causal, w_ref[g], 0.0).astype(jnp.bfloat16) for g in range(SGU_GROUPS)]

    n_chunks = u_ref.shape[0] // SGU_CHUNK
    for n in range(n_chunks):
        rows = slice(n * SGU_CHUNK, (n + 1) * SGU_CHUNK)
        for sl in range(SGU_WIDTH // LANES):
            cols = slice(sl * LANES, (sl + 1) * LANES)
            vs = v[rows, cols]
            lo = jnp.dot(weights[2 * sl], vs, preferred_element_type=f32)
            hi = jnp.dot(weights[2 * sl + 1], vs, preferred_element_type=f32)
            mixed = jnp.where(first_group, lo, hi) + b_ref[:, cols]
            o_ref[rows, cols] = (u[rows, cols] * mixed).astype(o_ref.dtype)


def _sgu(uv, ln_gain, ln_bias, w_spatial, b_mat):
    n_tok = uv.shape[0]
    tm = TOKEN_TILE
    n_slabs = SGU_WIDTH // LANES
    return pl.pallas_call(
        _sgu_kernel,
        grid=(n_tok // tm,),
        in_specs=[
            pl.BlockSpec((tm, SGU_WIDTH), lambda i: (i, 0)),
            pl.BlockSpec((tm, SGU_WIDTH), lambda i: (i, 1)),
            pl.BlockSpec((1, SGU_WIDTH), lambda i: (0, 0)),
            pl.BlockSpec((1, SGU_WIDTH), lambda i: (0, 0)),
            pl.BlockSpec((SGU_GROUPS, SGU_CHUNK, SGU_CHUNK), lambda i: (0, 0, 0)),
            pl.BlockSpec((SGU_CHUNK, SGU_WIDTH), lambda i: (0, 0)),
        ],
        out_specs=pl.BlockSpec((tm, SGU_WIDTH), lambda i: (i, 0)),
        out_shape=jax.ShapeDtypeStruct((n_tok, SGU_WIDTH), jnp.bfloat16),
        compiler_params=pltpu.CompilerParams(
            dimension_semantics=("arbitrary",), vmem_limit_bytes=VMEM_LIMIT_BYTES),
        name="sgu",
    )(uv, uv, ln_gain, ln_bias, w_spatial, b_mat)


def _out_ffn_kernel(x_ref, attn_ref, sgu_ref, ga_ref, gs_ref, wo_ref, gpm_ref, gpf_ref,
                    wg_ref, wu_ref, wd_ref, gpo_ref, o_ref):
    f32 = jnp.float32
    bf16 = jnp.bfloat16
    a = _rms(attn_ref[...].astype(f32), ga_ref[...]).astype(bf16)
    s = _rms(sgu_ref[...].astype(f32), gs_ref[...]).astype(bf16)
    y = (jnp.dot(a, wo_ref[0:ATTN_WIDTH, :], preferred_element_type=f32)
         + jnp.dot(s, wo_ref[ATTN_WIDTH:, :], preferred_element_type=f32))
    x1 = x_ref[...] + _rms(y, gpm_ref[...])
    h = _rms(x1, gpf_ref[...]).astype(bf16)
    gate = jnp.dot(h, wg_ref[...], preferred_element_type=f32)
    up = jnp.dot(h, wu_ref[...], preferred_element_type=f32)
    act = (gate * jax.nn.sigmoid(gate) * up).astype(bf16)
    f = jnp.dot(act, wd_ref[...], preferred_element_type=f32)
    o_ref[...] = x1 + _rms(f, gpo_ref[...])


def _out_ffn(x2, attn, sgu, ga, gs, wo, gpm, gpf, wg, wu, wd, gpo):
    n_tok = x2.shape[0]
    tm = 256

    def const(shape):
        return pl.BlockSpec(shape, lambda i: (0,) * len(shape), pipeline_mode=pl.Buffered(1))

    return pl.pallas_call(
        _out_ffn_kernel,
        grid=(n_tok // tm,),
        in_specs=[
            pl.BlockSpec((tm, D_MODEL), lambda i: (i, 0)),
            pl.BlockSpec((tm, ATTN_WIDTH), lambda i: (i, 0)),
            pl.BlockSpec((tm, SGU_WIDTH), lambda i: (i, 0)),
            const((1, ATTN_WIDTH)),
            const((1, SGU_WIDTH)),
            const((D_MODEL, D_MODEL)),
            const((1, D_MODEL)),
            const((1, D_MODEL)),
            const((D_MODEL, D_FF)),
            const((D_MODEL, D_FF)),
            const((D_FF, D_MODEL)),
            const((1, D_MODEL)),
        ],
        out_specs=pl.BlockSpec((tm, D_MODEL), lambda i: (i, 0)),
        out_shape=jax.ShapeDtypeStruct((n_tok, D_MODEL), jnp.float32),
        compiler_params=pltpu.CompilerParams(
            dimension_semantics=("arbitrary",), vmem_limit_bytes=VMEM_LIMIT_BYTES),
        name="out_ffn",
    )(x2, attn, sgu, ga, gs, wo, gpm, gpf, wg, wu, wd, gpo)


def kernel(x, positions, pre_mix_norm, w_in, sgu_ln_gain, sgu_ln_bias, sgu_w_spatial, sgu_b_spatial, attn_out_norm, sgu_out_norm, w_out, post_mix_norm, pre_ffn_norm, w_gate, w_up, w_down, post_ffn_norm):
    batch, seq, _ = x.shape
    bf16 = jnp.bfloat16
    n_tok = batch * seq
    x2 = x.reshape(n_tok, D_MODEL)
    pos2 = positions.reshape(n_tok, 1)

    inv_freq = ROPE_THETA ** (-jnp.arange(0, ROT_DIM, 2, dtype=jnp.float32) / ROT_DIM)
    lane = jnp.arange(LANES)
    in_rot = (lane % HEAD_DIM) < ROT_DIM
    invf = jnp.where(in_rot, inv_freq[lane % (ROT_DIM // 2)], 0.0).reshape(1, LANES)

    depth = w_in.shape[0]
    for l in range(depth):
        qkv, uv = _in_proj(x2, pos2, pre_mix_norm[l][None, :], invf, w_in[l].astype(bf16))
        attn = _attention(qkv, batch, seq)
        b_mat = jnp.repeat(sgu_b_spatial[l].T, SGU_WIDTH // SGU_GROUPS, axis=1)
        sgu = _sgu(uv, sgu_ln_gain[l][None, :], sgu_ln_bias[l][None, :], sgu_w_spatial[l], b_mat)
        x2 = _out_ffn(x2, attn, sgu, attn_out_norm[l][None, :], sgu_out_norm[l][None, :],
                      w_out[l].astype(bf16), post_mix_norm[l][None, :], pre_ffn_norm[l][None, :],
                      w_gate[l].astype(bf16), w_up[l].astype(bf16), w_down[l].astype(bf16),
                      post_ffn_norm[l][None, :])
    return x2.reshape(batch, seq, D_MODEL)
```

```python
import functools

import jax
import jax.numpy as jnp
import numpy as np
from jax import lax
from jax.experimental import pallas as pl
from jax.experimental.pallas import tpu as pltpu

D_MODEL = 1024
HEAD_DIM = 64
ATTN_WIDTH = 512
SGU_WIDTH = 512
SGU_GROUPS = 8
SGU_CHUNK = 128
DILATIONS = (1, 4, 16)
WINDOW_STEPS = 128
BLOCK_Q = 128
SCORE_SCALE = 0.125
ROPE_THETA = 500000.0
ROT_DIM = 16
D_FF = 2816
RMS_EPS = 1e-6
LN_EPS = 1e-5
MASK_VALUE = float(np.finfo(np.float32).min)

LANES = 128
VMEM_LIMIT_BYTES = 56 * 1024 * 1024

TOKEN_TILE = 512
IN_PROJ_TOKEN_TILE = 1024
IN_PROJ_SUBTILES = 4
FFN_TOKEN_TILE = 1024
FFN_SUBTILES = 4


def _rms(x, gain):
    return x * lax.rsqrt(jnp.mean(x * x, axis=-1, keepdims=True) + RMS_EPS) * gain


def _rotary_tables(pos_row, invf_col, spread):
    f32, bf16 = jnp.float32, jnp.bfloat16
    ang = invf_col * pos_row.astype(f32)
    trig = jnp.concatenate([jnp.cos(ang), jnp.sin(ang)], axis=0)
    hi = trig.astype(bf16)
    rest = trig - hi.astype(f32)
    mid = rest.astype(bf16)
    lo = (rest - mid.astype(f32)).astype(bf16)
    terms = jnp.concatenate([hi, mid, lo], axis=0)
    return lax.dot_general(terms, spread, (((0,), (0,)), ((), ())), preferred_element_type=f32)


def _in_proj_kernel(x_ref, pos_ref, gain_ref, invf_ref, spread_ref, w_ref, qkv_ref, uv_ref):
    tables = _rotary_tables(pos_ref[...], invf_ref[...], spread_ref[...])
    lane = lax.broadcasted_iota(jnp.int32, (1, LANES), 1)
    ones_off_rotary = jnp.where((lane % HEAD_DIM) < ROT_DIM, 0.0, 1.0)
    half = ROT_DIM // 2
    q_slabs = ATTN_WIDTH // LANES
    qk_width = 2 * ATTN_WIDTH

    sub = x_ref.shape[0] // IN_PROJ_SUBTILES
    for t in range(IN_PROJ_SUBTILES):
        rows = slice(t * sub, (t + 1) * sub)
        h = _rms(x_ref[rows, :], gain_ref[...]).astype(jnp.bfloat16)
        proj = jnp.dot(h, w_ref[...], preferred_element_type=jnp.float32)
        cos = tables[rows, :LANES] + ones_off_rotary
        sin_from_below = tables[rows, LANES:2 * LANES]
        sin_from_above = tables[rows, 2 * LANES:]
        for j in range(2 * q_slabs):
            slab = proj[:, j * LANES:(j + 1) * LANES]
            rot = (slab * cos
                   + pltpu.roll(slab, half, axis=1) * sin_from_below
                   + pltpu.roll(slab, LANES - half, axis=1) * sin_from_above)
            if j < q_slabs:
                rot = rot * SCORE_SCALE
            qkv_ref[rows, j * LANES:(j + 1) * LANES] = rot.astype(qkv_ref.dtype)
        qkv_ref[rows, qk_width:] = proj[:, qk_width:3 * ATTN_WIDTH].astype(qkv_ref.dtype)
        uv_ref[rows, :] = proj[:, 3 * ATTN_WIDTH:].astype(uv_ref.dtype)


def _rotary_constants():
    n_freq = ROT_DIM // 2
    inv_freq = ROPE_THETA ** (-jnp.arange(0, ROT_DIM, 2, dtype=jnp.float32) / ROT_DIM)
    lane = np.arange(LANES)
    in_rot = (lane % HEAD_DIM) < ROT_DIM
    upper = (lane % ROT_DIM) >= n_freq
    hit = (lane[None, :] % n_freq) == np.arange(n_freq)[:, None]
    cos_rows = np.concatenate([hit & in_rot, np.zeros_like(hit), np.zeros_like(hit)], axis=1)
    sin_rows = np.concatenate([np.zeros(hit.shape), 1.0 * (hit & in_rot & upper),
                               -1.0 * (hit & in_rot & ~upper)], axis=1)
    one_term = np.concatenate([cos_rows.astype(np.float32), sin_rows.astype(np.float32)], axis=0)
    spread = np.concatenate([one_term] * 3, axis=0)
    return inv_freq.reshape(n_freq, 1), jnp.asarray(spread, dtype=jnp.bfloat16)


def _in_proj(x2, pos_row, gain, w_in):
    n_tok = x2.shape[0]
    tm = IN_PROJ_TOKEN_TILE
    width = w_in.shape[1]
    invf, spread = _rotary_constants()
    return pl.pallas_call(
        _in_proj_kernel,
        grid=(n_tok // tm,),
        in_specs=[
            pl.BlockSpec((tm, D_MODEL), lambda i: (i, 0)),
            pl.BlockSpec((1, tm), lambda i: (0, i)),
            pl.BlockSpec((1, D_MODEL), lambda i: (0, 0)),
            pl.BlockSpec(invf.shape, lambda i: (0, 0)),
            pl.BlockSpec(spread.shape, lambda i: (0, 0)),
            pl.BlockSpec((D_MODEL, width), lambda i: (0, 0)),
        ],
        out_specs=[
            pl.BlockSpec((tm, 3 * ATTN_WIDTH), lambda i: (i, 0)),
            pl.BlockSpec((tm, 2 * SGU_WIDTH), lambda i: (i, 0)),
        ],
        out_shape=[
            jax.ShapeDtypeStruct((n_tok, 3 * ATTN_WIDTH), jnp.bfloat16),
            jax.ShapeDtypeStruct((n_tok, 2 * SGU_WIDTH), jnp.bfloat16),
        ],
        compiler_params=pltpu.CompilerParams(
            dimension_semantics=("arbitrary",), vmem_limit_bytes=VMEM_LIMIT_BYTES),
        name="in_proj",
    )(x2, pos_row, gain, invf, spread, w_in)


def _attn_kernel(q_ref, k_ref, v_ref, o_ref,
                 q32, k32, v32, qp, kp, vp, bias_ref, s_scr, m_scr, ot, mt, dt, *, seq):
    n_blocks = seq // BLOCK_Q
    n_pat = len(DILATIONS)
    f32 = jnp.float32
    bf16 = jnp.bfloat16

    q32[...] = q_ref[...].astype(f32)
    k32[...] = k_ref[...].astype(f32)
    v32[...] = v_ref[...].astype(f32)

    row = lax.broadcasted_iota(jnp.int32, (BLOCK_Q, 2 * BLOCK_Q), 0)
    col = lax.broadcasted_iota(jnp.int32, (BLOCK_Q, 2 * BLOCK_Q), 1)
    keep = (col >= row) & (col <= row + WINDOW_STEPS)
    bias_ref[...] = jnp.where(keep, 0.0, MASK_VALUE)

    lane = lax.broadcasted_iota(jnp.int32, (BLOCK_Q, LANES), 1)
    first_head = lane < HEAD_DIM

    def head_select(a, b):
        return jnp.where(first_head, a, b)

    for p, dil in enumerate(DILATIONS):
        sub_len = seq // dil
        blocks_per_sub = sub_len // BLOCK_Q
        if dil > 1:
            for r in range(dil):
                src = pl.ds(r, sub_len, stride=dil)
                dst = slice(r * sub_len, (r + 1) * sub_len)
                qp[dst, :] = q32[src, :].astype(bf16)
                kp[dst, :] = k32[src, :].astype(bf16)
                vp[dst, :] = v32[src, :].astype(bf16)
            q_src, k_src, v_src = qp, kp, vp
        else:
            q_src, k_src, v_src = q_ref, k_ref, v_ref

        def key_window(i):
            key_lo = i * BLOCK_Q if i % blocks_per_sub == 0 else (i - 1) * BLOCK_Q
            return key_lo, (i + 1) * BLOCK_Q - key_lo

        def scores(i):
            key_lo, n_keys = key_window(i)
            qb = q_src[i * BLOCK_Q:(i + 1) * BLOCK_Q, :]
            kw = k_src[key_lo:key_lo + n_keys, :]
            bias = bias_ref[:, 2 * BLOCK_Q - n_keys:]
            zero = jnp.zeros_like(qb)
            q2 = jnp.concatenate([jnp.where(first_head, qb, zero), jnp.where(first_head, zero, qb)], axis=0)
            s = lax.dot_general(q2, kw, (((1,), (1,)), ((), ())), preferred_element_type=f32)
            s = s + jnp.concatenate([bias, bias], axis=0)
            s_scr[i % 2, :, :n_keys] = s
            m_scr[i % 2] = jnp.broadcast_to(jnp.max(s, axis=-1, keepdims=True), (2 * BLOCK_Q, LANES))

        def weighted_values(i):
            key_lo, n_keys = key_window(i)
            r, j = divmod(i, blocks_per_sub)
            vw = v_src[key_lo:key_lo + n_keys, :]
            m = m_scr[i % 2]
            s = s_scr[i % 2, :, :n_keys]
            e = jnp.exp(s - jnp.concatenate([m] * (n_keys // LANES), axis=1)).astype(bf16)
            v_ext = jnp.concatenate([vw, jnp.ones((n_keys, LANES), bf16)], axis=1)
            o2 = jnp.dot(e, v_ext, preferred_element_type=f32)
            if dil > 1:
                dst = pl.ds(r + dil * BLOCK_Q * j, BLOCK_Q, stride=dil)
            else:
                dst = pl.ds(i * BLOCK_Q, BLOCK_Q)
            ot[p, dst, :] = head_select(o2[:BLOCK_Q, :LANES], o2[BLOCK_Q:, :LANES])
            dt[p, dst, :] = head_select(o2[:BLOCK_Q, LANES:], o2[BLOCK_Q:, LANES:])
            mt[p, dst, :] = head_select(m[:BLOCK_Q], m[BLOCK_Q:])

        scores(0)
        for i in range(n_blocks):
            if i + 1 < n_blocks:
                scores(i + 1)
            weighted_values(i)

    for i in range(n_blocks):
        rows = slice(i * BLOCK_Q, (i + 1) * BLOCK_Q)
        maxes = [mt[p, rows, :] for p in range(n_pat)]
        m_all = jnp.maximum(jnp.maximum(maxes[0], maxes[1]), maxes[2])
        num = jnp.zeros((BLOCK_Q, LANES), f32)
        den = jnp.zeros((BLOCK_Q, LANES), f32)
        for p in range(n_pat):
            w = jnp.exp(maxes[p] - m_all)
            num = num + w * ot[p, rows, :]
            den = den + w * dt[p, rows, :]
        o_ref[rows, :] = (num / den).astype(o_ref.dtype)


def _attention(qkv, batch, seq):
    n_tok = qkv.shape[0]
    n_pairs = ATTN_WIDTH // LANES
    f32, bf16 = jnp.float32, jnp.bfloat16
    n_pat = len(DILATIONS)
    return pl.pallas_call(
        functools.partial(_attn_kernel, seq=seq),
        grid=(batch, n_pairs),
        in_specs=[
            pl.BlockSpec((seq, LANES), lambda b, h: (b, h)),
            pl.BlockSpec((seq, LANES), lambda b, h: (b, n_pairs + h)),
            pl.BlockSpec((seq, LANES), lambda b, h: (b, 2 * n_pairs + h)),
        ],
        out_specs=pl.BlockSpec((seq, LANES), lambda b, h: (b, h)),
        out_shape=jax.ShapeDtypeStruct((n_tok, ATTN_WIDTH), bf16),
        scratch_shapes=[
            pltpu.VMEM((seq, LANES), f32),
            pltpu.VMEM((seq, LANES), f32),
            pltpu.VMEM((seq, LANES), f32),
            pltpu.VMEM((seq, LANES), bf16),
            pltpu.VMEM((seq, LANES), bf16),
            pltpu.VMEM((seq, LANES), bf16),
            pltpu.VMEM((BLOCK_Q, 2 * BLOCK_Q), f32),
            pltpu.VMEM((2, 2 * BLOCK_Q, 2 * BLOCK_Q), f32),
            pltpu.VMEM((2, 2 * BLOCK_Q, LANES), f32),
            pltpu.VMEM((n_pat, seq, LANES), f32),
            pltpu.VMEM((n_pat, seq, LANES), f32),
            pltpu.VMEM((n_pat, seq, LANES), f32),
        ],
        compiler_params=pltpu.CompilerParams(
            dimension_semantics=("arbitrary", "arbitrary"), vmem_limit_bytes=VMEM_LIMIT_BYTES),
        name="attention",
    )(qkv, qkv, qkv)


def _gelu(x):
    return 0.5 * x * (1.0 + lax.erf(x * np.float32(1.0 / np.sqrt(2.0))))


def _sgu_kernel(u_ref, v_ref, gain_ref, bias_ref, w_ref, b_ref, o_ref):
    f32 = jnp.float32
    u = _gelu(u_ref[...].astype(f32))
    v = _gelu(v_ref[...].astype(f32))
    mu = jnp.mean(v, axis=-1, keepdims=True)
    vc = v - mu
    v = vc * lax.rsqrt(jnp.mean(vc * vc, axis=-1, keepdims=True) + LN_EPS)
    v = (v * gain_ref[...] + bias_ref[...]).astype(jnp.bfloat16)

    row = lax.broadcasted_iota(jnp.int32, (SGU_CHUNK, SGU_CHUNK), 0)
    col = lax.broadcasted_iota(jnp.int32, (SGU_CHUNK, SGU_CHUNK), 1)
    causal = col <= row
    lane = lax.broadcasted_iota(jnp.int32, (SGU_CHUNK, LANES), 1)
    first_group = lane < (SGU_WIDTH // SGU_GROUPS)
    weights = [jnp.where(causal, w_ref[g], 0.0).astype(jnp.bfloat16) for g in range(SGU_GROUPS)]

    n_chunks = u_ref.shape[0] // SGU_CHUNK
    for n in range(n_chunks):
        rows = slice(n * SGU_CHUNK, (n + 1) * SGU_CHUNK)
        for sl in range(SGU_WIDTH // LANES):
            cols = slice(sl * LANES, (sl + 1) * LANES)
            vs = v[rows, cols]
            lo = jnp.dot(weights[2 * sl], vs, preferred_element_type=f32)
            hi = jnp.dot(weights[2 * sl + 1], vs, preferred_element_type=f32)
            mixed = jnp.where(first_group, lo, hi) + b_ref[:, cols]
            o_ref[rows, cols] = (u[rows, cols] * mixed).astype(o_ref.dtype)


def _sgu(uv, ln_gain, ln_bias, w_spatial, b_mat):
    n_tok = uv.shape[0]
    tm = TOKEN_TILE
    return pl.pallas_call(
        _sgu_kernel,
        grid=(n_tok // tm,),
        in_specs=[
            pl.BlockSpec((tm, SGU_WIDTH), lambda i: (i, 0)),
            pl.BlockSpec((tm, SGU_WIDTH), lambda i: (i, 1)),
            pl.BlockSpec((1, SGU_WIDTH), lambda i: (0, 0)),
            pl.BlockSpec((1, SGU_WIDTH), lambda i: (0, 0)),
            pl.BlockSpec((SGU_GROUPS, SGU_CHUNK, SGU_CHUNK), lambda i: (0, 0, 0)),
            pl.BlockSpec((SGU_CHUNK, SGU_WIDTH), lambda i: (0, 0)),
        ],
        out_specs=pl.BlockSpec((tm, SGU_WIDTH), lambda i: (i, 0)),
        out_shape=jax.ShapeDtypeStruct((n_tok, SGU_WIDTH), jnp.bfloat16),
        compiler_params=pltpu.CompilerParams(
            dimension_semantics=("arbitrary",), vmem_limit_bytes=VMEM_LIMIT_BYTES),
        name="sgu",
    )(uv, uv, ln_gain, ln_bias, w_spatial, b_mat)


def _out_ffn_kernel(x_ref, attn_ref, sgu_ref, ga_ref, gs_ref, wo_ref, gpm_ref, gpf_ref,
                    wg_ref, wu_ref, wd_ref, gpo_ref, o_ref):
    f32 = jnp.float32
    bf16 = jnp.bfloat16
    sub = x_ref.shape[0] // FFN_SUBTILES
    row_slices = [slice(t * sub, (t + 1) * sub) for t in range(FFN_SUBTILES)]

    def out_proj(rows):
        a = _rms(attn_ref[rows, :].astype(f32), ga_ref[...]).astype(bf16)
        s = _rms(sgu_ref[rows, :].astype(f32), gs_ref[...]).astype(bf16)
        return (jnp.dot(a, wo_ref[0:ATTN_WIDTH, :], preferred_element_type=f32)
                + jnp.dot(s, wo_ref[ATTN_WIDTH:, :], preferred_element_type=f32))

    def ffn(rows, y):
        x1 = x_ref[rows, :] + _rms(y, gpm_ref[...])
        h = _rms(x1, gpf_ref[...]).astype(bf16)
        gate = jnp.dot(h, wg_ref[...], preferred_element_type=f32)
        up = jnp.dot(h, wu_ref[...], preferred_element_type=f32)
        act = (gate * jax.nn.sigmoid(gate) * up).astype(bf16)
        return x1, jnp.dot(act, wd_ref[...], preferred_element_type=f32)

    y_next = out_proj(row_slices[0])
    for t, rows in enumerate(row_slices):
        y = y_next
        if t + 1 < FFN_SUBTILES:
            y_next = out_proj(row_slices[t + 1])
        x1, f = ffn(rows, y)
        o_ref[rows, :] = x1 + _rms(f, gpo_ref[...])


def _out_ffn(x2, attn, sgu, ga, gs, wo, gpm, gpf, wg, wu, wd, gpo):
    n_tok = x2.shape[0]
    tm = FFN_TOKEN_TILE

    def const(shape):
        return pl.BlockSpec(shape, lambda i: (0,) * len(shape), pipeline_mode=pl.Buffered(1))

    return pl.pallas_call(
        _out_ffn_kernel,
        grid=(n_tok // tm,),
        in_specs=[
            pl.BlockSpec((tm, D_MODEL), lambda i: (i, 0)),
            pl.BlockSpec((tm, ATTN_WIDTH), lambda i: (i, 0)),
            pl.BlockSpec((tm, SGU_WIDTH), lambda i: (i, 0)),
            const((1, ATTN_WIDTH)),
            const((1, SGU_WIDTH)),
            const((D_MODEL, D_MODEL)),
            const((1, D_MODEL)),
            const((1, D_MODEL)),
            const((D_MODEL, D_FF)),
            const((D_MODEL, D_FF)),
            const((D_FF, D_MODEL)),
            const((1, D_MODEL)),
        ],
        out_specs=pl.BlockSpec((tm, D_MODEL), lambda i: (i, 0)),
        out_shape=jax.ShapeDtypeStruct((n_tok, D_MODEL), jnp.float32),
        compiler_params=pltpu.CompilerParams(
            dimension_semantics=("arbitrary",), vmem_limit_bytes=VMEM_LIMIT_BYTES),
        name="out_ffn",
    )(x2, attn, sgu, ga, gs, wo, gpm, gpf, wg, wu, wd, gpo)


def kernel(x, positions, pre_mix_norm, w_in, sgu_ln_gain, sgu_ln_bias, sgu_w_spatial, sgu_b_spatial, attn_out_norm, sgu_out_norm, w_out, post_mix_norm, pre_ffn_norm, w_gate, w_up, w_down, post_ffn_norm):
    batch, seq, _ = x.shape
    bf16 = jnp.bfloat16
    n_tok = batch * seq
    x2 = x.reshape(n_tok, D_MODEL)
    pos_row = positions.reshape(1, n_tok)

    depth = w_in.shape[0]
    for l in range(depth):
        qkv, uv = _in_proj(x2, pos_row, pre_mix_norm[l][None, :], w_in[l].astype(bf16))
        attn = _attention(qkv, batch, seq)
        b_mat = jnp.repeat(sgu_b_spatial[l].T, SGU_WIDTH // SGU_GROUPS, axis=1)
        sgu = _sgu(uv, sgu_ln_gain[l][None, :], sgu_ln_bias[l][None, :], sgu_w_spatial[l], b_mat)
        x2 = _out_ffn(x2, attn, sgu, attn_out_norm[l][None, :], sgu_out_norm[l][None, :],
                      w_out[l].astype(bf16), post_mix_norm[l][None, :], pre_ffn_norm[l][None, :],
                      w_gate[l].astype(bf16), w_up[l].astype(bf16), w_down[l].astype(bf16),
                      post_ffn_norm[l][None, :])
    return x2.reshape(batch, seq, D_MODEL)
```

```python
import functools

import jax
import jax.numpy as jnp
import numpy as np
from jax import lax
from jax.experimental import pallas as pl
from jax.experimental.pallas import tpu as pltpu

D_MODEL = 1024
HEAD_DIM = 64
ATTN_WIDTH = 512
SGU_WIDTH = 512
SGU_GROUPS = 8
SGU_CHUNK = 128
DILATIONS = (1, 4, 16)
WINDOW_STEPS = 128
BLOCK_Q = 128
DILATION_STEP = 4
ATTN_LOOKAHEAD = 3
SCORE_SCALE = float(np.float32(0.125 * np.log2(np.e)))
ROPE_THETA = 500000.0
ROT_DIM = 16
D_FF = 2816
RMS_EPS = 1e-6
LN_EPS = 1e-5
MASK_VALUE = float(np.finfo(np.float32).min)

LANES = 128
VMEM_LIMIT_BYTES = 56 * 1024 * 1024

TOKEN_TILE = 512
IN_PROJ_TOKEN_TILE = 1024
IN_PROJ_SUBTILES = 4
FFN_TOKEN_TILE = 1024
FFN_SUBTILES = 4


def _rms(x, gain):
    return x * lax.rsqrt(jnp.mean(x * x, axis=-1, keepdims=True) + RMS_EPS) * gain


def _rotary_tables(pos_row, invf_col, spread):
    f32, bf16 = jnp.float32, jnp.bfloat16
    ang = invf_col * pos_row.astype(f32)
    trig = jnp.concatenate([jnp.cos(ang), jnp.sin(ang)], axis=0)
    hi = trig.astype(bf16)
    rest = trig - hi.astype(f32)
    mid = rest.astype(bf16)
    lo = (rest - mid.astype(f32)).astype(bf16)
    terms = jnp.concatenate([hi, mid, lo], axis=0)
    return lax.dot_general(terms, spread, (((0,), (0,)), ((), ())), preferred_element_type=f32)


def _in_proj_kernel(x_ref, pos_ref, gain_ref, invf_ref, spread_ref, w_ref, qkv_ref, uv_ref):
    tables = _rotary_tables(pos_ref[...], invf_ref[...], spread_ref[...])
    lane = lax.broadcasted_iota(jnp.int32, (1, LANES), 1)
    ones_off_rotary = jnp.where((lane % HEAD_DIM) < ROT_DIM, 0.0, 1.0)
    half = ROT_DIM // 2
    q_slabs = ATTN_WIDTH // LANES
    qk_width = 2 * ATTN_WIDTH

    sub = x_ref.shape[0] // IN_PROJ_SUBTILES
    for t in range(IN_PROJ_SUBTILES):
        rows = slice(t * sub, (t + 1) * sub)
        h = _rms(x_ref[rows, :], gain_ref[...]).astype(jnp.bfloat16)
        proj = jnp.dot(h, w_ref[...], preferred_element_type=jnp.float32)
        cos = tables[rows, :LANES] + ones_off_rotary
        sin_from_below = tables[rows, LANES:2 * LANES]
        sin_from_above = tables[rows, 2 * LANES:]
        for j in range(2 * q_slabs):
            slab = proj[:, j * LANES:(j + 1) * LANES]
            rot = (slab * cos
                   + pltpu.roll(slab, half, axis=1) * sin_from_below
                   + pltpu.roll(slab, LANES - half, axis=1) * sin_from_above)
            if j < q_slabs:
                rot = rot * SCORE_SCALE
            qkv_ref[rows, j * LANES:(j + 1) * LANES] = rot.astype(qkv_ref.dtype)
        qkv_ref[rows, qk_width:] = proj[:, qk_width:3 * ATTN_WIDTH].astype(qkv_ref.dtype)
        uv_ref[rows, :] = proj[:, 3 * ATTN_WIDTH:].astype(uv_ref.dtype)


def _rotary_constants():
    n_freq = ROT_DIM // 2
    inv_freq = ROPE_THETA ** (-jnp.arange(0, ROT_DIM, 2, dtype=jnp.float32) / ROT_DIM)
    lane = np.arange(LANES)
    in_rot = (lane % HEAD_DIM) < ROT_DIM
    upper = (lane % ROT_DIM) >= n_freq
    hit = (lane[None, :] % n_freq) == np.arange(n_freq)[:, None]
    cos_rows = np.concatenate([hit & in_rot, np.zeros_like(hit), np.zeros_like(hit)], axis=1)
    sin_rows = np.concatenate([np.zeros(hit.shape), 1.0 * (hit & in_rot & upper),
                               -1.0 * (hit & in_rot & ~upper)], axis=1)
    one_term = np.concatenate([cos_rows.astype(np.float32), sin_rows.astype(np.float32)], axis=0)
    spread = np.concatenate([one_term] * 3, axis=0)
    return inv_freq.reshape(n_freq, 1), jnp.asarray(spread, dtype=jnp.bfloat16)


def _in_proj(x2, pos_row, gain, w_in):
    n_tok = x2.shape[0]
    tm = IN_PROJ_TOKEN_TILE
    width = w_in.shape[1]
    invf, spread = _rotary_constants()
    return pl.pallas_call(
        _in_proj_kernel,
        grid=(n_tok // tm,),
        in_specs=[
            pl.BlockSpec((tm, D_MODEL), lambda i: (i, 0)),
            pl.BlockSpec((1, tm), lambda i: (0, i)),
            pl.BlockSpec((1, D_MODEL), lambda i: (0, 0)),
            pl.BlockSpec(invf.shape, lambda i: (0, 0)),
            pl.BlockSpec(spread.shape, lambda i: (0, 0)),
            pl.BlockSpec((D_MODEL, width), lambda i: (0, 0)),
        ],
        out_specs=[
            pl.BlockSpec((tm, 3 * ATTN_WIDTH), lambda i: (i, 0)),
            pl.BlockSpec((tm, 2 * SGU_WIDTH), lambda i: (i, 0)),
        ],
        out_shape=[
            jax.ShapeDtypeStruct((n_tok, 3 * ATTN_WIDTH), jnp.bfloat16),
            jax.ShapeDtypeStruct((n_tok, 2 * SGU_WIDTH), jnp.bfloat16),
        ],
        compiler_params=pltpu.CompilerParams(
            dimension_semantics=("arbitrary",), vmem_limit_bytes=VMEM_LIMIT_BYTES),
        name="in_proj",
    )(x2, pos_row, gain, invf, spread, w_in)


def _attn_kernel(q_ref, k_ref, v_ref, o_ref,
                 q32, k32, v32, q32g, k32g, v32g, qp1, kp1, vp1, qp2, kp2, vp2,
                 bias_ref, ot, mt, dt, out32, *, seq):
    n_blocks = seq // BLOCK_Q
    n_pat = len(DILATIONS)
    f32 = jnp.float32
    bf16 = jnp.bfloat16

    row = lax.broadcasted_iota(jnp.int32, (BLOCK_Q, 2 * BLOCK_Q), 0)
    col = lax.broadcasted_iota(jnp.int32, (BLOCK_Q, 2 * BLOCK_Q), 1)
    keep = (col >= row) & (col <= row + WINDOW_STEPS)
    bias_ref[...] = jnp.where(keep, 0.0, MASK_VALUE)

    lane = lax.broadcasted_iota(jnp.int32, (BLOCK_Q, LANES), 1)
    first_head = lane < HEAD_DIM

    def head_select(a, b):
        return jnp.where(first_head, a, b)

    def regroup(src, chunk, dst32, dst16):
        quarter = chunk // DILATION_STEP
        for c in range(seq // chunk):
            for r in range(DILATION_STEP):
                rows = src[pl.ds(c * chunk + r, quarter, stride=DILATION_STEP), :]
                lo = c * chunk + r * quarter
                if dst32 is not None:
                    dst32[lo:lo + quarter, :] = rows
                dst16[lo:lo + quarter, :] = rows.astype(bf16)

    sources = [(q_ref, k_ref, v_ref), (qp1, kp1, vp1), (qp2, kp2, vp2)]
    for x_ref, x32, x32g, xp1, xp2 in ((q_ref, q32, q32g, qp1, qp2), (k_ref, k32, k32g, kp1, kp2),
                                       (v_ref, v32, v32g, vp1, vp2)):
        x32[...] = x_ref[...].astype(f32)
        regroup(x32, seq, x32g, xp1)
        regroup(x32g, seq // DILATION_STEP, None, xp2)

    def level1_rows(level, i):
        if level < 2:
            return pl.ds(i * BLOCK_Q, BLOCK_Q)
        c, r = divmod(i, DILATION_STEP)
        return pl.ds(c * DILATION_STEP * BLOCK_Q + r, BLOCK_Q, stride=DILATION_STEP)

    def key_window(level, i):
        blocks_per_sub = n_blocks // DILATIONS[level]
        key_lo = i * BLOCK_Q if i % blocks_per_sub == 0 else (i - 1) * BLOCK_Q
        return key_lo, (i + 1) * BLOCK_Q - key_lo

    def probabilities(level, i):
        q_src, k_src, _ = sources[level]
        key_lo, n_keys = key_window(level, i)
        qb = q_src[i * BLOCK_Q:(i + 1) * BLOCK_Q, :]
        kw = k_src[key_lo:key_lo + n_keys, :]
        bias = bias_ref[:, 2 * BLOCK_Q - n_keys:]
        zero = jnp.zeros_like(qb)
        q2 = jnp.concatenate([jnp.where(first_head, qb, zero), jnp.where(first_head, zero, qb)], axis=0)
        s = lax.dot_general(q2, kw, (((1,), (1,)), ((), ())), preferred_element_type=f32)
        s = s + jnp.concatenate([bias, bias], axis=0)
        m = jnp.max(s, axis=-1, keepdims=True)
        return jnp.exp2(s - m).astype(bf16), m

    def weighted_values(level, i, e, m):
        key_lo, n_keys = key_window(level, i)
        vw = sources[level][2][key_lo:key_lo + n_keys, :]
        v_ext = jnp.concatenate([vw, jnp.ones((n_keys, LANES), bf16)], axis=1)
        o2 = jnp.dot(e, v_ext, preferred_element_type=f32)
        dst = level1_rows(level, i)
        m_wide = jnp.broadcast_to(m, (2 * BLOCK_Q, LANES))
        ot[level, dst, :] = head_select(o2[:BLOCK_Q, :LANES], o2[BLOCK_Q:, :LANES])
        dt[level, dst, :] = head_select(o2[:BLOCK_Q, LANES:], o2[BLOCK_Q:, LANES:])
        mt[level, dst, :] = head_select(m_wide[:BLOCK_Q], m_wide[BLOCK_Q:])

    quarter_blocks = n_blocks // DILATION_STEP

    def merge(b):
        r, j = divmod(b, quarter_blocks)
        rows0 = pl.ds(r + DILATION_STEP * BLOCK_Q * j, BLOCK_Q, stride=DILATION_STEP)
        rows1 = pl.ds(b * BLOCK_Q, BLOCK_Q)
        rows = [rows0] + [rows1] * (n_pat - 1)
        maxes = [mt[p, rows[p], :] for p in range(n_pat)]
        m_all = functools.reduce(jnp.maximum, maxes)
        num = jnp.zeros((BLOCK_Q, LANES), f32)
        den = jnp.zeros((BLOCK_Q, LANES), f32)
        for p in range(n_pat):
            w = jnp.exp2(maxes[p] - m_all)
            num = num + w * ot[p, rows[p], :]
            den = den + w * dt[p, rows[p], :]
        out32[rows0, :] = num / den

    order = [(0, i) for i in range(n_blocks)]
    for b in range(n_blocks):
        order += [(1, b), (2, b)]
    pending = {}
    for n in range(len(order) + ATTN_LOOKAHEAD):
        if n < len(order):
            pending[order[n]] = probabilities(*order[n])
        if n >= ATTN_LOOKAHEAD:
            level, i = order[n - ATTN_LOOKAHEAD]
            weighted_values(level, i, *pending.pop((level, i)))
            if level == 2 and i % DILATION_STEP == DILATION_STEP - 1:
                for b in range(i + 1 - DILATION_STEP, i + 1):
                    merge(b)
    o_ref[...] = out32[...].astype(o_ref.dtype)


def _attention(qkv, batch, seq):
    n_tok = qkv.shape[0]
    n_pairs = ATTN_WIDTH // LANES
    f32, bf16 = jnp.float32, jnp.bfloat16
    n_pat = len(DILATIONS)
    assert all(d == DILATION_STEP ** n for n, d in enumerate(DILATIONS)) and n_pat == 3
    assert seq % (DILATIONS[-1] * BLOCK_Q) == 0
    return pl.pallas_call(
        functools.partial(_attn_kernel, seq=seq),
        grid=(batch, n_pairs),
        in_specs=[
            pl.BlockSpec((seq, LANES), lambda b, h: (b, h)),
            pl.BlockSpec((seq, LANES), lambda b, h: (b, n_pairs + h)),
            pl.BlockSpec((seq, LANES), lambda b, h: (b, 2 * n_pairs + h)),
        ],
        out_specs=pl.BlockSpec((seq, LANES), lambda b, h: (b, h)),
        out_shape=jax.ShapeDtypeStruct((n_tok, ATTN_WIDTH), bf16),
        scratch_shapes=(
            [pltpu.VMEM((seq, LANES), f32)] * 6
            + [pltpu.VMEM((seq, LANES), bf16)] * 6
            + [pltpu.VMEM((BLOCK_Q, 2 * BLOCK_Q), f32)]
            + [pltpu.VMEM((n_pat, seq, LANES), f32)] * 3
            + [pltpu.VMEM((seq, LANES), f32)]
        ),
        compiler_params=pltpu.CompilerParams(
            dimension_semantics=("arbitrary", "arbitrary"), vmem_limit_bytes=VMEM_LIMIT_BYTES),
        name="attention",
    )(qkv, qkv, qkv)


def _gelu(x):
    return 0.5 * x * (1.0 + lax.erf(x * np.float32(1.0 / np.sqrt(2.0))))


def _sgu_kernel(u_ref, v_ref, gain_ref, bias_ref, w_ref, b_ref, o_ref):
    f32 = jnp.float32
    u = _gelu(u_ref[...].astype(f32))
    v = _gelu(v_ref[...].astype(f32))
    mu = jnp.mean(v, axis=-1, keepdims=True)
    vc = v - mu
    v = vc * lax.rsqrt(jnp.mean(vc * vc, axis=-1, keepdims=True) + LN_EPS)
    v = (v * gain_ref[...] + bias_ref[...]).astype(jnp.bfloat16)

    row = lax.broadcasted_iota(jnp.int32, (SGU_CHUNK, SGU_CHUNK), 0)
    col = lax.broadcasted_iota(jnp.int32, (SGU_CHUNK, SGU_CHUNK), 1)
    causal = col <= row
    lane = lax.broadcasted_iota(jnp.int32, (SGU_CHUNK, LANES), 1)
    first_group = lane < (SGU_WIDTH // SGU_GROUPS)
    weights = [jnp.where(causal, w_ref[g], 0.0).astype(jnp.bfloat16) for g in range(SGU_GROUPS)]

    n_chunks = u_ref.shape[0] // SGU_CHUNK
    for n in range(n_chunks):
        rows = slice(n * SGU_CHUNK, (n + 1) * SGU_CHUNK)
        for sl in range(SGU_WIDTH // LANES):
            cols = slice(sl * LANES, (sl + 1) * LANES)
            vs = v[rows, cols]
            lo = jnp.dot(weights[2 * sl], vs, preferred_element_type=f32)
            hi = jnp.dot(weights[2 * sl + 1], vs, preferred_element_type=f32)
            mixed = jnp.where(first_group, lo, hi) + b_ref[:, cols]
            o_ref[rows, cols] = (u[rows, cols] * mixed).astype(o_ref.dtype)


def _sgu(uv, ln_gain, ln_bias, w_spatial, b_mat):
    n_tok = uv.shape[0]
    tm = TOKEN_TILE
    return pl.pallas_call(
        _sgu_kernel,
        grid=(n_tok // tm,),
        in_specs=[
            pl.BlockSpec((tm, SGU_WIDTH), lambda i: (i, 0)),
            pl.BlockSpec((tm, SGU_WIDTH), lambda i: (i, 1)),
            pl.BlockSpec((1, SGU_WIDTH), lambda i: (0, 0)),
            pl.BlockSpec((1, SGU_WIDTH), lambda i: (0, 0)),
            pl.BlockSpec((SGU_GROUPS, SGU_CHUNK, SGU_CHUNK), lambda i: (0, 0, 0)),
            pl.BlockSpec((SGU_CHUNK, SGU_WIDTH), lambda i: (0, 0)),
        ],
        out_specs=pl.BlockSpec((tm, SGU_WIDTH), lambda i: (i, 0)),
        out_shape=jax.ShapeDtypeStruct((n_tok, SGU_WIDTH), jnp.bfloat16),
        compiler_params=pltpu.CompilerParams(
            dimension_semantics=("arbitrary",), vmem_limit_bytes=VMEM_LIMIT_BYTES),
        name="sgu",
    )(uv, uv, ln_gain, ln_bias, w_spatial, b_mat)


def _out_ffn_kernel(x_ref, attn_ref, sgu_ref, ga_ref, gs_ref, wo_ref, gpm_ref, gpf_ref,
                    wg_ref, wu_ref, wd_ref, gpo_ref, o_ref):
    f32 = jnp.float32
    bf16 = jnp.bfloat16
    sub = x_ref.shape[0] // FFN_SUBTILES
    row_slices = [slice(t * sub, (t + 1) * sub) for t in range(FFN_SUBTILES)]

    def out_proj(rows):
        a = _rms(attn_ref[rows, :].astype(f32), ga_ref[...]).astype(bf16)
        s = _rms(sgu_ref[rows, :].astype(f32), gs_ref[...]).astype(bf16)
        return (jnp.dot(a, wo_ref[0:ATTN_WIDTH, :], preferred_element_type=f32)
                + jnp.dot(s, wo_ref[ATTN_WIDTH:, :], preferred_element_type=f32))

    def ffn(rows, y):
        x1 = x_ref[rows, :] + _rms(y, gpm_ref[...])
        h = _rms(x1, gpf_ref[...]).astype(bf16)
        gate = jnp.dot(h, wg_ref[...], preferred_element_type=f32)
        up = jnp.dot(h, wu_ref[...], preferred_element_type=f32)
        act = (gate * jax.nn.sigmoid(gate) * up).astype(bf16)
        return x1, jnp.dot(act, wd_ref[...], preferred_element_type=f32)

    y_next = out_proj(row_slices[0])
    for t, rows in enumerate(row_slices):
        y = y_next
        if t + 1 < FFN_SUBTILES:
            y_next = out_proj(row_slices[t + 1])
        x1, f = ffn(rows, y)
        o_ref[rows, :] = x1 + _rms(f, gpo_ref[...])


def _out_ffn(x2, attn, sgu, ga, gs, wo, gpm, gpf, wg, wu, wd, gpo):
    n_tok = x2.shape[0]
    tm = FFN_TOKEN_TILE

    def const(shape):
        return pl.BlockSpec(shape, lambda i: (0,) * len(shape), pipeline_mode=pl.Buffered(1))

    return pl.pallas_call(
        _out_ffn_kernel,
        grid=(n_tok // tm,),
        in_specs=[
            pl.BlockSpec((tm, D_MODEL), lambda i: (i, 0)),
            pl.BlockSpec((tm, ATTN_WIDTH), lambda i: (i, 0)),
            pl.BlockSpec((tm, SGU_WIDTH), lambda i: (i, 0)),
            const((1, ATTN_WIDTH)),
            const((1, SGU_WIDTH)),
            const((D_MODEL, D_MODEL)),
            const((1, D_MODEL)),
            const((1, D_MODEL)),
            const((D_MODEL, D_FF)),
            const((D_MODEL, D_FF)),
            const((D_FF, D_MODEL)),
            const((1, D_MODEL)),
        ],
        out_specs=pl.BlockSpec((tm, D_MODEL), lambda i: (i, 0)),
        out_shape=jax.ShapeDtypeStruct((n_tok, D_MODEL), jnp.float32),
        compiler_params=pltpu.CompilerParams(
            dimension_semantics=("arbitrary",), vmem_limit_bytes=VMEM_LIMIT_BYTES),
        name="out_ffn",
    )(x2, attn, sgu, ga, gs, wo, gpm, gpf, wg, wu, wd, gpo)


def kernel(x, positions, pre_mix_norm, w_in, sgu_ln_gain, sgu_ln_bias, sgu_w_spatial, sgu_b_spatial, attn_out_norm, sgu_out_norm, w_out, post_mix_norm, pre_ffn_norm, w_gate, w_up, w_down, post_ffn_norm):
    batch, seq, _ = x.shape
    bf16 = jnp.bfloat16
    n_tok = batch * seq
    x2 = x.reshape(n_tok, D_MODEL)
    pos_row = positions.reshape(1, n_tok)

    depth = w_in.shape[0]
    for l in range(depth):
        qkv, uv = _in_proj(x2, pos_row, pre_mix_norm[l][None, :], w_in[l].astype(bf16))
        attn = _attention(qkv, batch, seq)
        b_mat = jnp.repeat(sgu_b_spatial[l].T, SGU_WIDTH // SGU_GROUPS, axis=1)
        sgu = _sgu(uv, sgu_ln_gain[l][None, :], sgu_ln_bias[l][None, :], sgu_w_spatial[l], b_mat)
        x2 = _out_ffn(x2, attn, sgu, attn_out_norm[l][None, :], sgu_out_norm[l][None, :],
                      w_out[l].astype(bf16), post_mix_norm[l][None, :], pre_ffn_norm[l][None, :],
                      w_gate[l].astype(bf16), w_up[l].astype(bf16), w_down[l].astype(bf16),
                      post_ffn_norm[l][None, :])
    return x2.reshape(batch, seq, D_MODEL)
```

```python
import functools

import jax
import jax.numpy as jnp
import numpy as np
from jax import lax
from jax.experimental import pallas as pl
from jax.experimental.pallas import tpu as pltpu

D_MODEL = 1024
HEAD_DIM = 64
ATTN_WIDTH = 512
SGU_WIDTH = 512
SGU_GROUPS = 8
SGU_CHUNK = 128
DILATIONS = (1, 4, 16)
WINDOW_STEPS = 128
BLOCK_Q = 128
DILATION_STEP = 4
ATTN_LOOKAHEAD = 3
SCORE_SCALE = float(np.float32(0.125 * np.log2(np.e)))
ROPE_THETA = 500000.0
ROT_DIM = 16
D_FF = 2816
RMS_EPS = 1e-6
LN_EPS = 1e-5
MASK_VALUE = float(np.finfo(np.float32).min)

LANES = 128
VMEM_LIMIT_BYTES = 56 * 1024 * 1024

TOKEN_TILE = 512
IN_PROJ_TOKEN_TILE = 1024
IN_PROJ_SUBTILES = 4
FFN_TOKEN_TILE = 1024
FFN_SUBTILES = 4


def _rms(x, gain):
    return x * lax.rsqrt(jnp.mean(x * x, axis=-1, keepdims=True) + RMS_EPS) * gain


def _rotary_tables(pos_row, invf_col, spread):
    f32, bf16 = jnp.float32, jnp.bfloat16
    ang = invf_col * pos_row.astype(f32)
    trig = jnp.concatenate([jnp.cos(ang), jnp.sin(ang)], axis=0)
    hi = trig.astype(bf16)
    rest = trig - hi.astype(f32)
    mid = rest.astype(bf16)
    lo = (rest - mid.astype(f32)).astype(bf16)
    terms = jnp.concatenate([hi, mid, lo], axis=0)
    return lax.dot_general(terms, spread, (((0,), (0,)), ((), ())), preferred_element_type=f32)


def _in_proj_kernel(x_ref, pos_ref, gain_ref, invf_ref, spread_ref, w_ref,
                    ln_gain_ref, ln_bias_ref, ws_ref, bs_ref, qkv_ref, sgu_ref):
    tables = _rotary_tables(pos_ref[...], invf_ref[...], spread_ref[...])
    pair_weights = _sgu_pair_weights(ws_ref)
    lane = lax.broadcasted_iota(jnp.int32, (1, LANES), 1)
    ones_off_rotary = jnp.where((lane % HEAD_DIM) < ROT_DIM, 0.0, 1.0)
    half = ROT_DIM // 2
    q_slabs = ATTN_WIDTH // LANES
    qk_width = 2 * ATTN_WIDTH
    uv_lo = 3 * ATTN_WIDTH

    sub = x_ref.shape[0] // IN_PROJ_SUBTILES
    row_slices = [slice(t * sub, (t + 1) * sub) for t in range(IN_PROJ_SUBTILES)]

    def project(rows):
        h = _rms(x_ref[rows, :], gain_ref[...]).astype(jnp.bfloat16)
        return jnp.dot(h, w_ref[...], preferred_element_type=jnp.float32)

    proj_next = project(row_slices[0])
    for t, rows in enumerate(row_slices):
        proj = proj_next
        if t + 1 < IN_PROJ_SUBTILES:
            proj_next = project(row_slices[t + 1])
        cos = tables[rows, :LANES] + ones_off_rotary
        sin_from_below = tables[rows, LANES:2 * LANES]
        sin_from_above = tables[rows, 2 * LANES:]
        for j in range(2 * q_slabs):
            slab = proj[:, j * LANES:(j + 1) * LANES]
            rot = (slab * cos
                   + pltpu.roll(slab, half, axis=1) * sin_from_below
                   + pltpu.roll(slab, LANES - half, axis=1) * sin_from_above)
            if j < q_slabs:
                rot = rot * SCORE_SCALE
            qkv_ref[rows, j * LANES:(j + 1) * LANES] = rot.astype(qkv_ref.dtype)
        qkv_ref[rows, qk_width:] = proj[:, qk_width:uv_lo].astype(qkv_ref.dtype)
        sgu = _sgu_rows(proj[:, uv_lo:uv_lo + SGU_WIDTH], proj[:, uv_lo + SGU_WIDTH:],
                        ln_gain_ref[...], ln_bias_ref[...], pair_weights, bs_ref[...])
        sgu_ref[rows, :] = sgu.astype(sgu_ref.dtype)


def _rotary_constants():
    n_freq = ROT_DIM // 2
    inv_freq = ROPE_THETA ** (-jnp.arange(0, ROT_DIM, 2, dtype=jnp.float32) / ROT_DIM)
    lane = np.arange(LANES)
    in_rot = (lane % HEAD_DIM) < ROT_DIM
    upper = (lane % ROT_DIM) >= n_freq
    hit = (lane[None, :] % n_freq) == np.arange(n_freq)[:, None]
    cos_rows = np.concatenate([hit & in_rot, np.zeros_like(hit), np.zeros_like(hit)], axis=1)
    sin_rows = np.concatenate([np.zeros(hit.shape), 1.0 * (hit & in_rot & upper),
                               -1.0 * (hit & in_rot & ~upper)], axis=1)
    one_term = np.concatenate([cos_rows.astype(np.float32), sin_rows.astype(np.float32)], axis=0)
    spread = np.concatenate([one_term] * 3, axis=0)
    return inv_freq.reshape(n_freq, 1), jnp.asarray(spread, dtype=jnp.bfloat16)


def _in_proj(x2, pos_row, gain, w_in, ln_gain, ln_bias, w_spatial, b_mat):
    n_tok = x2.shape[0]
    tm = IN_PROJ_TOKEN_TILE
    width = w_in.shape[1]
    invf, spread = _rotary_constants()
    assert (tm // IN_PROJ_SUBTILES) % SGU_CHUNK == 0

    def const(shape):
        return pl.BlockSpec(shape, lambda i: (0,) * len(shape))

    return pl.pallas_call(
        _in_proj_kernel,
        grid=(n_tok // tm,),
        in_specs=[
            pl.BlockSpec((tm, D_MODEL), lambda i: (i, 0)),
            pl.BlockSpec((1, tm), lambda i: (0, i)),
            const((1, D_MODEL)),
            const(invf.shape),
            const(spread.shape),
            const((D_MODEL, width)),
            const((1, SGU_WIDTH)),
            const((1, SGU_WIDTH)),
            const((SGU_GROUPS, SGU_CHUNK, SGU_CHUNK)),
            const((SGU_CHUNK, SGU_WIDTH)),
        ],
        out_specs=[
            pl.BlockSpec((tm, 3 * ATTN_WIDTH), lambda i: (i, 0)),
            pl.BlockSpec((tm, SGU_WIDTH), lambda i: (i, 0)),
        ],
        out_shape=[
            jax.ShapeDtypeStruct((n_tok, 3 * ATTN_WIDTH), jnp.bfloat16),
            jax.ShapeDtypeStruct((n_tok, SGU_WIDTH), jnp.bfloat16),
        ],
        compiler_params=pltpu.CompilerParams(
            dimension_semantics=("arbitrary",), vmem_limit_bytes=VMEM_LIMIT_BYTES),
        name="in_proj",
    )(x2, pos_row, gain, invf, spread, w_in, ln_gain, ln_bias, w_spatial, b_mat)


def _attn_kernel(q_ref, k_ref, v_ref, o_ref,
                 q32, k32, v32, q32g, k32g, v32g, qp1, kp1, vp1, qp2, kp2, vp2,
                 bias_ref, ot, mt, dt, out32, *, seq):
    n_blocks = seq // BLOCK_Q
    n_pat = len(DILATIONS)
    f32 = jnp.float32
    bf16 = jnp.bfloat16

    row = lax.broadcasted_iota(jnp.int32, (BLOCK_Q, 2 * BLOCK_Q), 0)
    col = lax.broadcasted_iota(jnp.int32, (BLOCK_Q, 2 * BLOCK_Q), 1)
    keep = (col >= row) & (col <= row + WINDOW_STEPS)
    bias_ref[...] = jnp.where(keep, 0.0, MASK_VALUE)

    lane = lax.broadcasted_iota(jnp.int32, (BLOCK_Q, LANES), 1)
    first_head = lane < HEAD_DIM

    def head_select(a, b):
        return jnp.where(first_head, a, b)

    def regroup(src, chunk, dst32, dst16):
        quarter = chunk // DILATION_STEP
        for c in range(seq // chunk):
            for r in range(DILATION_STEP):
                rows = src[pl.ds(c * chunk + r, quarter, stride=DILATION_STEP), :]
                lo = c * chunk + r * quarter
                if dst32 is not None:
                    dst32[lo:lo + quarter, :] = rows
                dst16[lo:lo + quarter, :] = rows.astype(bf16)

    sources = [(q_ref, k_ref, v_ref), (qp1, kp1, vp1), (qp2, kp2, vp2)]
    for x_ref, x32, x32g, xp1, xp2 in ((q_ref, q32, q32g, qp1, qp2), (k_ref, k32, k32g, kp1, kp2),
                                       (v_ref, v32, v32g, vp1, vp2)):
        x32[...] = x_ref[...].astype(f32)
        regroup(x32, seq, x32g, xp1)
        regroup(x32g, seq // DILATION_STEP, None, xp2)

    def level1_rows(level, i):
        if level < 2:
            return pl.ds(i * BLOCK_Q, BLOCK_Q)
        c, r = divmod(i, DILATION_STEP)
        return pl.ds(c * DILATION_STEP * BLOCK_Q + r, BLOCK_Q, stride=DILATION_STEP)

    def key_window(level, i):
        blocks_per_sub = n_blocks // DILATIONS[level]
        key_lo = i * BLOCK_Q if i % blocks_per_sub == 0 else (i - 1) * BLOCK_Q
        return key_lo, (i + 1) * BLOCK_Q - key_lo

    def probabilities(level, i):
        q_src, k_src, _ = sources[level]
        key_lo, n_keys = key_window(level, i)
        qb = q_src[i * BLOCK_Q:(i + 1) * BLOCK_Q, :]
        kw = k_src[key_lo:key_lo + n_keys, :]
        bias = bias_ref[:, 2 * BLOCK_Q - n_keys:]
        zero = jnp.zeros_like(qb)
        q2 = jnp.concatenate([jnp.where(first_head, qb, zero), jnp.where(first_head, zero, qb)], axis=0)
        s = lax.dot_general(q2, kw, (((1,), (1,)), ((), ())), preferred_element_type=f32)
        s = s + jnp.concatenate([bias, bias], axis=0)
        m = jnp.max(s, axis=-1, keepdims=True)
        return jnp.exp2(s - m).astype(bf16), m

    def weighted_values(level, i, e, m):
        key_lo, n_keys = key_window(level, i)
        vw = sources[level][2][key_lo:key_lo + n_keys, :]
        v_ext = jnp.concatenate([vw, jnp.ones((n_keys, LANES), bf16)], axis=1)
        o2 = jnp.dot(e, v_ext, preferred_element_type=f32)
        dst = level1_rows(level, i)
        m_wide = jnp.broadcast_to(m, (2 * BLOCK_Q, LANES))
        ot[level, dst, :] = head_select(o2[:BLOCK_Q, :LANES], o2[BLOCK_Q:, :LANES])
        dt[level, dst, :] = head_select(o2[:BLOCK_Q, LANES:], o2[BLOCK_Q:, LANES:])
        mt[level, dst, :] = head_select(m_wide[:BLOCK_Q], m_wide[BLOCK_Q:])

    quarter_blocks = n_blocks // DILATION_STEP

    def merge(b):
        r, j = divmod(b, quarter_blocks)
        rows0 = pl.ds(r + DILATION_STEP * BLOCK_Q * j, BLOCK_Q, stride=DILATION_STEP)
        rows1 = pl.ds(b * BLOCK_Q, BLOCK_Q)
        rows = [rows0] + [rows1] * (n_pat - 1)
        maxes = [mt[p, rows[p], :] for p in range(n_pat)]
        m_all = functools.reduce(jnp.maximum, maxes)
        num = jnp.zeros((BLOCK_Q, LANES), f32)
        den = jnp.zeros((BLOCK_Q, LANES), f32)
        for p in range(n_pat):
            w = jnp.exp2(maxes[p] - m_all)
            num = num + w * ot[p, rows[p], :]
            den = den + w * dt[p, rows[p], :]
        out32[rows0, :] = num / den

    order = [(0, i) for i in range(n_blocks)]
    for b in range(n_blocks):
        order += [(1, b), (2, b)]
    pending = {}
    for n in range(len(order) + ATTN_LOOKAHEAD):
        if n < len(order):
            pending[order[n]] = probabilities(*order[n])
        if n >= ATTN_LOOKAHEAD:
            level, i = order[n - ATTN_LOOKAHEAD]
            weighted_values(level, i, *pending.pop((level, i)))
            if level == 2 and i % DILATION_STEP == DILATION_STEP - 1:
                for b in range(i + 1 - DILATION_STEP, i + 1):
                    merge(b)
    o_ref[...] = out32[...].astype(o_ref.dtype)


def _attention(qkv, batch, seq):
    n_tok = qkv.shape[0]
    n_pairs = ATTN_WIDTH // LANES
    f32, bf16 = jnp.float32, jnp.bfloat16
    n_pat = len(DILATIONS)
    assert all(d == DILATION_STEP ** n for n, d in enumerate(DILATIONS)) and n_pat == 3
    assert seq % (DILATIONS[-1] * BLOCK_Q) == 0
    return pl.pallas_call(
        functools.partial(_attn_kernel, seq=seq),
        grid=(batch, n_pairs),
        in_specs=[
            pl.BlockSpec((seq, LANES), lambda b, h: (b, h)),
            pl.BlockSpec((seq, LANES), lambda b, h: (b, n_pairs + h)),
            pl.BlockSpec((seq, LANES), lambda b, h: (b, 2 * n_pairs + h)),
        ],
        out_specs=pl.BlockSpec((seq, LANES), lambda b, h: (b, h)),
        out_shape=jax.ShapeDtypeStruct((n_tok, ATTN_WIDTH), bf16),
        scratch_shapes=(
            [pltpu.VMEM((seq, LANES), f32)] * 6
            + [pltpu.VMEM((seq, LANES), bf16)] * 6
            + [pltpu.VMEM((BLOCK_Q, 2 * BLOCK_Q), f32)]
            + [pltpu.VMEM((n_pat, seq, LANES), f32)] * 3
            + [pltpu.VMEM((seq, LANES), f32)]
        ),
        compiler_params=pltpu.CompilerParams(
            dimension_semantics=("arbitrary", "arbitrary"), vmem_limit_bytes=VMEM_LIMIT_BYTES),
        name="attention",
    )(qkv, qkv, qkv)


def _gelu(x):
    return 0.5 * x * (1.0 + lax.erf(x * np.float32(1.0 / np.sqrt(2.0))))


def _sgu_pair_weights(w_ref):
    row = lax.broadcasted_iota(jnp.int32, (SGU_CHUNK, SGU_CHUNK), 0)
    col = lax.broadcasted_iota(jnp.int32, (SGU_CHUNK, SGU_CHUNK), 1)
    weights = [jnp.where(col <= row, w_ref[g], 0.0).astype(jnp.bfloat16) for g in range(SGU_GROUPS)]
    return [jnp.concatenate(weights[2 * sl:2 * sl + 2], axis=1) for sl in range(SGU_WIDTH // LANES)]


def _sgu_rows(u, v, gain, bias, pair_weights, b_mat):
    f32 = jnp.float32
    u = _gelu(u.astype(f32))
    v = _gelu(v.astype(f32))
    mu = jnp.mean(v, axis=-1, keepdims=True)
    vc = v - mu
    v = vc * lax.rsqrt(jnp.mean(vc * vc, axis=-1, keepdims=True) + LN_EPS)
    v = (v * gain + bias).astype(jnp.bfloat16)

    lane = lax.broadcasted_iota(jnp.int32, (SGU_CHUNK, LANES), 1)
    first_group = lane < (SGU_WIDTH // SGU_GROUPS)
    chunks = []
    for n in range(u.shape[0] // SGU_CHUNK):
        rows = slice(n * SGU_CHUNK, (n + 1) * SGU_CHUNK)
        slabs = []
        for sl in range(SGU_WIDTH // LANES):
            cols = slice(sl * LANES, (sl + 1) * LANES)
            vs = v[rows, cols]
            zero = jnp.zeros_like(vs)
            v_diag = jnp.concatenate([jnp.where(first_group, vs, zero), jnp.where(first_group, zero, vs)], axis=0)
            mixed = jnp.dot(pair_weights[sl], v_diag, preferred_element_type=f32) + b_mat[:, cols]
            slabs.append(u[rows, cols] * mixed)
        chunks.append(jnp.concatenate(slabs, axis=1))
    return jnp.concatenate(chunks, axis=0)


def _out_ffn_kernel(x_ref, attn_ref, sgu_ref, ga_ref, gs_ref, wo_ref, gpm_ref, gpf_ref,
                    wg_ref, wu_ref, wd_ref, gpo_ref, o_ref):
    f32 = jnp.float32
    bf16 = jnp.bfloat16
    sub = x_ref.shape[0] // FFN_SUBTILES
    row_slices = [slice(t * sub, (t + 1) * sub) for t in range(FFN_SUBTILES)]

    def out_proj(rows):
        a = _rms(attn_ref[rows, :].astype(f32), ga_ref[...]).astype(bf16)
        s = _rms(sgu_ref[rows, :].astype(f32), gs_ref[...]).astype(bf16)
        return (jnp.dot(a, wo_ref[0:ATTN_WIDTH, :], preferred_element_type=f32)
                + jnp.dot(s, wo_ref[ATTN_WIDTH:, :], preferred_element_type=f32))

    def ffn(rows, y):
        x1 = x_ref[rows, :] + _rms(y, gpm_ref[...])
        h = _rms(x1, gpf_ref[...]).astype(bf16)
        gate = jnp.dot(h, wg_ref[...], preferred_element_type=f32)
        up = jnp.dot(h, wu_ref[...], preferred_element_type=f32)
        act = (gate * jax.nn.sigmoid(gate) * up).astype(bf16)
        return x1, jnp.dot(act, wd_ref[...], preferred_element_type=f32)

    y_next = out_proj(row_slices[0])
    for t, rows in enumerate(row_slices):
        y = y_next
        if t + 1 < FFN_SUBTILES:
            y_next = out_proj(row_slices[t + 1])
        x1, f = ffn(rows, y)
        o_ref[rows, :] = x1 + _rms(f, gpo_ref[...])


def _out_ffn(x2, attn, sgu, ga, gs, wo, gpm, gpf, wg, wu, wd, gpo):
    n_tok = x2.shape[0]
    tm = FFN_TOKEN_TILE

    def const(shape):
        return pl.BlockSpec(shape, lambda i: (0,) * len(shape), pipeline_mode=pl.Buffered(1))

    return pl.pallas_call(
        _out_ffn_kernel,
        grid=(n_tok // tm,),
        in_specs=[
            pl.BlockSpec((tm, D_MODEL), lambda i: (i, 0)),
            pl.BlockSpec((tm, ATTN_WIDTH), lambda i: (i, 0)),
            pl.BlockSpec((tm, SGU_WIDTH), lambda i: (i, 0)),
            const((1, ATTN_WIDTH)),
            const((1, SGU_WIDTH)),
            const((D_MODEL, D_MODEL)),
            const((1, D_MODEL)),
            const((1, D_MODEL)),
            const((D_MODEL, D_FF)),
            const((D_MODEL, D_FF)),
            const((D_FF, D_MODEL)),
            const((1, D_MODEL)),
        ],
        out_specs=pl.BlockSpec((tm, D_MODEL), lambda i: (i, 0)),
        out_shape=jax.ShapeDtypeStruct((n_tok, D_MODEL), jnp.float32),
        compiler_params=pltpu.CompilerParams(
            dimension_semantics=("arbitrary",), vmem_limit_bytes=VMEM_LIMIT_BYTES),
        name="out_ffn",
    )(x2, attn, sgu, ga, gs, wo, gpm, gpf, wg, wu, wd, gpo)


def kernel(x, positions, pre_mix_norm, w_in, sgu_ln_gain, sgu_ln_bias, sgu_w_spatial, sgu_b_spatial, attn_out_norm, sgu_out_norm, w_out, post_mix_norm, pre_ffn_norm, w_gate, w_up, w_down, post_ffn_norm):
    batch, seq, _ = x.shape
    bf16 = jnp.bfloat16
    n_tok = batch * seq
    x2 = x.reshape(n_tok, D_MODEL)
    pos_row = positions.reshape(1, n_tok)

    depth = w_in.shape[0]
    for l in range(depth):
        b_mat = jnp.repeat(sgu_b_spatial[l].T, SGU_WIDTH // SGU_GROUPS, axis=1)
        qkv, sgu = _in_proj(x2, pos_row, pre_mix_norm[l][None, :], w_in[l].astype(bf16),
                            sgu_ln_gain[l][None, :], sgu_ln_bias[l][None, :], sgu_w_spatial[l], b_mat)
        attn = _attention(qkv, batch, seq)
        x2 = _out_ffn(x2, attn, sgu, attn_out_norm[l][None, :], sgu_out_norm[l][None, :],
                      w_out[l].astype(bf16), post_mix_norm[l][None, :], pre_ffn_norm[l][None, :],
                      w_gate[l].astype(bf16), w_up[l].astype(bf16), w_down[l].astype(bf16),
                      post_ffn_norm[l][None, :])
    return x2.reshape(batch, seq, D_MODEL)
```

```python
import functools

import jax
import jax.numpy as jnp
import numpy as np
from jax import lax
from jax.experimental import pallas as pl
from jax.experimental.pallas import tpu as pltpu

D_MODEL = 1024
HEAD_DIM = 64
ATTN_WIDTH = 512
SGU_WIDTH = 512
SGU_GROUPS = 8
SGU_CHUNK = 128
DILATIONS = (1, 4, 16)
WINDOW_STEPS = 128
BLOCK_Q = 128
DILATION_STEP = 4
ATTN_LOOKAHEAD = 3
SCORE_SCALE = float(np.float32(0.125 * np.log2(np.e)))
ROPE_THETA = 500000.0
ROT_DIM = 16
D_FF = 2816
RMS_EPS = 1e-6
LN_EPS = 1e-5
MASK_VALUE = float(np.finfo(np.float32).min)

LANES = 128
BF16_SUBLANE_TILE = 16
VMEM_LIMIT_BYTES = 56 * 1024 * 1024

IN_PROJ_TOKEN_TILE = 1024
IN_PROJ_SUBTILES = 4
FFN_TOKEN_TILE = 1024
FFN_SUBTILES = 4


def _rms(x, gain):
    return x * lax.rsqrt(jnp.mean(x * x, axis=-1, keepdims=True) + RMS_EPS) * gain


def _rotary_tables(pos_row, invf_col, spread):
    f32, bf16 = jnp.float32, jnp.bfloat16
    ang = invf_col * pos_row.astype(f32)
    trig = jnp.concatenate([jnp.cos(ang), jnp.sin(ang)], axis=0)
    hi = trig.astype(bf16)
    rest = trig - hi.astype(f32)
    mid = rest.astype(bf16)
    lo = (rest - mid.astype(f32)).astype(bf16)
    terms = jnp.concatenate([hi, mid, lo], axis=0)
    return lax.dot_general(terms, spread, (((0,), (0,)), ((), ())), preferred_element_type=f32)


def _in_proj_kernel(x_ref, pos_ref, gain_ref, invf_ref, spread_ref, w32_ref,
                    ln_gain_ref, ln_bias_ref, ws_ref, bs_ref, *rest):
    n_cast = len(rest) // 2 - 1
    cast_src, (qkv_ref, sgu_ref), cast_dst, w_ref = (
        rest[:n_cast], rest[n_cast:n_cast + 2], rest[n_cast + 2:2 * n_cast + 2], rest[-1])

    @pl.when(pl.program_id(0) == 0)
    def _():
        w_ref[...] = w32_ref[...].astype(w_ref.dtype)

    for src, dst in zip(cast_src, cast_dst):
        dst[...] = src[...].astype(dst.dtype)

    tables = _rotary_tables(pos_ref[...], invf_ref[...], spread_ref[...])
    pair_weights = _sgu_pair_weights(ws_ref)
    lane = lax.broadcasted_iota(jnp.int32, (1, LANES), 1)
    ones_off_rotary = jnp.where((lane % HEAD_DIM) < ROT_DIM, 0.0, 1.0)
    half = ROT_DIM // 2
    q_slabs = ATTN_WIDTH // LANES
    qk_width = 2 * ATTN_WIDTH
    uv_lo = 3 * ATTN_WIDTH

    sub = x_ref.shape[0] // IN_PROJ_SUBTILES
    row_slices = [slice(t * sub, (t + 1) * sub) for t in range(IN_PROJ_SUBTILES)]

    def project(rows):
        h = _rms(x_ref[rows, :], gain_ref[...]).astype(jnp.bfloat16)
        return jnp.dot(h, w_ref[...], preferred_element_type=jnp.float32)

    proj_next = project(row_slices[0])
    for t, rows in enumerate(row_slices):
        proj = proj_next
        if t + 1 < IN_PROJ_SUBTILES:
            proj_next = project(row_slices[t + 1])
        cos = tables[rows, :LANES] + ones_off_rotary
        sin_from_below = tables[rows, LANES:2 * LANES]
        sin_from_above = tables[rows, 2 * LANES:]
        for j in range(2 * q_slabs):
            slab = proj[:, j * LANES:(j + 1) * LANES]
            rot = (slab * cos
                   + pltpu.roll(slab, half, axis=1) * sin_from_below
                   + pltpu.roll(slab, LANES - half, axis=1) * sin_from_above)
            if j < q_slabs:
                rot = rot * SCORE_SCALE
            qkv_ref[rows, j * LANES:(j + 1) * LANES] = rot.astype(qkv_ref.dtype)
        qkv_ref[rows, qk_width:] = proj[:, qk_width:uv_lo].astype(qkv_ref.dtype)
        sgu = _sgu_rows(proj[:, uv_lo:uv_lo + SGU_WIDTH], proj[:, uv_lo + SGU_WIDTH:],
                        ln_gain_ref[...], ln_bias_ref[...], pair_weights, bs_ref[...])
        sgu_ref[rows, :] = sgu.astype(sgu_ref.dtype)


def _rotary_constants():
    n_freq = ROT_DIM // 2
    inv_freq = ROPE_THETA ** (-jnp.arange(0, ROT_DIM, 2, dtype=jnp.float32) / ROT_DIM)
    lane = np.arange(LANES)
    in_rot = (lane % HEAD_DIM) < ROT_DIM
    upper = (lane % ROT_DIM) >= n_freq
    hit = (lane[None, :] % n_freq) == np.arange(n_freq)[:, None]
    cos_rows = np.concatenate([hit & in_rot, np.zeros_like(hit), np.zeros_like(hit)], axis=1)
    sin_rows = np.concatenate([np.zeros(hit.shape), 1.0 * (hit & in_rot & upper),
                               -1.0 * (hit & in_rot & ~upper)], axis=1)
    one_term = np.concatenate([cos_rows.astype(np.float32), sin_rows.astype(np.float32)], axis=0)
    spread = np.concatenate([one_term] * 3, axis=0)
    return inv_freq.reshape(n_freq, 1), jnp.asarray(spread, dtype=jnp.bfloat16)


def _in_proj(x2, pos_row, gain, w_in, ln_gain, ln_bias, w_spatial, b_mat, later_weights):
    n_tok = x2.shape[0]
    tm = IN_PROJ_TOKEN_TILE
    n_steps = n_tok // tm
    width = w_in.shape[1]
    bf16 = jnp.bfloat16
    invf, spread = _rotary_constants()
    assert (tm // IN_PROJ_SUBTILES) % SGU_CHUNK == 0

    def const(shape, **kwargs):
        return pl.BlockSpec(shape, lambda i: (0,) * len(shape), **kwargs)

    def row_slab(w):
        rows = w.shape[0] // n_steps
        assert rows * n_steps == w.shape[0] and rows % BF16_SUBLANE_TILE == 0
        return pl.BlockSpec((rows, w.shape[1]), lambda i: (i, 0))

    slabs = [row_slab(w) for w in later_weights]
    outs = pl.pallas_call(
        _in_proj_kernel,
        grid=(n_steps,),
        in_specs=[
            pl.BlockSpec((tm, D_MODEL), lambda i: (i, 0)),
            pl.BlockSpec((1, tm), lambda i: (0, i)),
            const((1, D_MODEL)),
            const(invf.shape),
            const(spread.shape),
            const((D_MODEL, width), pipeline_mode=pl.Buffered(1)),
            const((1, SGU_WIDTH)),
            const((1, SGU_WIDTH)),
            const((SGU_GROUPS, SGU_CHUNK, SGU_CHUNK)),
            const((SGU_CHUNK, SGU_WIDTH)),
        ] + slabs,
        out_specs=[
            pl.BlockSpec((tm, 3 * ATTN_WIDTH), lambda i: (i, 0)),
            pl.BlockSpec((tm, SGU_WIDTH), lambda i: (i, 0)),
        ] + slabs,
        out_shape=[
            jax.ShapeDtypeStruct((n_tok, 3 * ATTN_WIDTH), bf16),
            jax.ShapeDtypeStruct((n_tok, SGU_WIDTH), bf16),
        ] + [jax.ShapeDtypeStruct(w.shape, bf16) for w in later_weights],
        scratch_shapes=[pltpu.VMEM((D_MODEL, width), bf16)],
        compiler_params=pltpu.CompilerParams(
            dimension_semantics=("arbitrary",), vmem_limit_bytes=VMEM_LIMIT_BYTES),
        name="in_proj",
    )(x2, pos_row, gain, invf, spread, w_in, ln_gain, ln_bias, w_spatial, b_mat, *later_weights)
    return outs[0], outs[1], outs[2:]


def _attn_kernel(q_ref, k_ref, v_ref, o_ref,
                 q32, k32, v32, q32g, k32g, v32g, qp1, kp1, vp1, qp2, kp2, vp2,
                 bias_ref, ot, mt, dt, out32, *, seq):
    n_blocks = seq // BLOCK_Q
    n_pat = len(DILATIONS)
    f32 = jnp.float32
    bf16 = jnp.bfloat16

    row = lax.broadcasted_iota(jnp.int32, (BLOCK_Q, 2 * BLOCK_Q), 0)
    col = lax.broadcasted_iota(jnp.int32, (BLOCK_Q, 2 * BLOCK_Q), 1)
    keep = (col >= row) & (col <= row + WINDOW_STEPS)
    bias_ref[...] = jnp.where(keep, 0.0, MASK_VALUE)

    lane = lax.broadcasted_iota(jnp.int32, (BLOCK_Q, LANES), 1)
    first_head = lane < HEAD_DIM

    def head_select(a, b):
        return jnp.where(first_head, a, b)

    def regroup(src, chunk, dst32, dst16):
        quarter = chunk // DILATION_STEP
        for c in range(seq // chunk):
            for r in range(DILATION_STEP):
                rows = src[pl.ds(c * chunk + r, quarter, stride=DILATION_STEP), :]
                lo = c * chunk + r * quarter
                if dst32 is not None:
                    dst32[lo:lo + quarter, :] = rows
                dst16[lo:lo + quarter, :] = rows.astype(bf16)

    sources = [(q_ref, k_ref, v_ref), (qp1, kp1, vp1), (qp2, kp2, vp2)]
    for x_ref, x32, x32g, xp1, xp2 in ((q_ref, q32, q32g, qp1, qp2), (k_ref, k32, k32g, kp1, kp2),
                                       (v_ref, v32, v32g, vp1, vp2)):
        x32[...] = x_ref[...].astype(f32)
        regroup(x32, seq, x32g, xp1)
        regroup(x32g, seq // DILATION_STEP, None, xp2)

    def level1_rows(level, i):
        if level < 2:
            return pl.ds(i * BLOCK_Q, BLOCK_Q)
        c, r = divmod(i, DILATION_STEP)
        return pl.ds(c * DILATION_STEP * BLOCK_Q + r, BLOCK_Q, stride=DILATION_STEP)

    def key_window(level, i):
        blocks_per_sub = n_blocks // DILATIONS[level]
        key_lo = i * BLOCK_Q if i % blocks_per_sub == 0 else (i - 1) * BLOCK_Q
        return key_lo, (i + 1) * BLOCK_Q - key_lo

    def probabilities(level, i):
        q_src, k_src, _ = sources[level]
        key_lo, n_keys = key_window(level, i)
        qb = q_src[i * BLOCK_Q:(i + 1) * BLOCK_Q, :]
        kw = k_src[key_lo:key_lo + n_keys, :]
        bias = bias_ref[:, 2 * BLOCK_Q - n_keys:]
        zero = jnp.zeros_like(qb)
        q2 = jnp.concatenate([jnp.where(first_head, qb, zero), jnp.where(first_head, zero, qb)], axis=0)
        s = lax.dot_general(q2, kw, (((1,), (1,)), ((), ())), preferred_element_type=f32)
        s = s + jnp.concatenate([bias, bias], axis=0)
        m = jnp.max(s, axis=-1, keepdims=True)
        return jnp.exp2(s - m).astype(bf16), m

    def weighted_values(level, i, e, m):
        key_lo, n_keys = key_window(level, i)
        vw = sources[level][2][key_lo:key_lo + n_keys, :]
        v_ext = jnp.concatenate([vw, jnp.ones((n_keys, LANES), bf16)], axis=1)
        o2 = jnp.dot(e, v_ext, preferred_element_type=f32)
        dst = level1_rows(level, i)
        m_wide = jnp.broadcast_to(m, (2 * BLOCK_Q, LANES))
        ot[level, dst, :] = head_select(o2[:BLOCK_Q, :LANES], o2[BLOCK_Q:, :LANES])
        dt[level, dst, :] = head_select(o2[:BLOCK_Q, LANES:], o2[BLOCK_Q:, LANES:])
        mt[level, dst, :] = head_select(m_wide[:BLOCK_Q], m_wide[BLOCK_Q:])

    quarter_blocks = n_blocks // DILATION_STEP

    def merge(b):
        r, j = divmod(b, quarter_blocks)
        rows0 = pl.ds(r + DILATION_STEP * BLOCK_Q * j, BLOCK_Q, stride=DILATION_STEP)
        rows1 = pl.ds(b * BLOCK_Q, BLOCK_Q)
        rows = [rows0] + [rows1] * (n_pat - 1)
        maxes = [mt[p, rows[p], :] for p in range(n_pat)]
        m_all = functools.reduce(jnp.maximum, maxes)
        num = jnp.zeros((BLOCK_Q, LANES), f32)
        den = jnp.zeros((BLOCK_Q, LANES), f32)
        for p in range(n_pat):
            w = jnp.exp2(maxes[p] - m_all)
            num = num + w * ot[p, rows[p], :]
            den = den + w * dt[p, rows[p], :]
        out32[rows0, :] = num / den

    order = [(0, i) for i in range(n_blocks)]
    for b in range(n_blocks):
        order += [(1, b), (2, b)]
    pending = {}
    for n in range(len(order) + ATTN_LOOKAHEAD):
        if n < len(order):
            pending[order[n]] = probabilities(*order[n])
        if n >= ATTN_LOOKAHEAD:
            level, i = order[n - ATTN_LOOKAHEAD]
            weighted_values(level, i, *pending.pop((level, i)))
            if level == 2 and i % DILATION_STEP == DILATION_STEP - 1:
                for b in range(i + 1 - DILATION_STEP, i + 1):
                    merge(b)
    o_ref[...] = out32[...].astype(o_ref.dtype)


def _attention(qkv, batch, seq):
    n_tok = qkv.shape[0]
    n_pairs = ATTN_WIDTH // LANES
    f32, bf16 = jnp.float32, jnp.bfloat16
    n_pat = len(DILATIONS)
    assert all(d == DILATION_STEP ** n for n, d in enumerate(DILATIONS)) and n_pat == 3
    assert seq % (DILATIONS[-1] * BLOCK_Q) == 0
    return pl.pallas_call(
        functools.partial(_attn_kernel, seq=seq),
        grid=(batch, n_pairs),
        in_specs=[
            pl.BlockSpec((seq, LANES), lambda b, h: (b, h)),
            pl.BlockSpec((seq, LANES), lambda b, h: (b, n_pairs + h)),
            pl.BlockSpec((seq, LANES), lambda b, h: (b, 2 * n_pairs + h)),
        ],
        out_specs=pl.BlockSpec((seq, LANES), lambda b, h: (b, h)),
        out_shape=jax.ShapeDtypeStruct((n_tok, ATTN_WIDTH), bf16),
        scratch_shapes=(
            [pltpu.VMEM((seq, LANES), f32)] * 6
            + [pltpu.VMEM((seq, LANES), bf16)] * 6
            + [pltpu.VMEM((BLOCK_Q, 2 * BLOCK_Q), f32)]
            + [pltpu.VMEM((n_pat, seq, LANES), f32)] * 3
            + [pltpu.VMEM((seq, LANES), f32)]
        ),
        compiler_params=pltpu.CompilerParams(
            dimension_semantics=("arbitrary", "arbitrary"), vmem_limit_bytes=VMEM_LIMIT_BYTES),
        name="attention",
    )(qkv, qkv, qkv)


def _gelu(x):
    return 0.5 * x * (1.0 + lax.erf(x * np.float32(1.0 / np.sqrt(2.0))))


def _sgu_pair_weights(w_ref):
    row = lax.broadcasted_iota(jnp.int32, (SGU_CHUNK, SGU_CHUNK), 0)
    col = lax.broadcasted_iota(jnp.int32, (SGU_CHUNK, SGU_CHUNK), 1)
    weights = [jnp.where(col <= row, w_ref[g], 0.0).astype(jnp.bfloat16) for g in range(SGU_GROUPS)]
    return [jnp.concatenate(weights[2 * sl:2 * sl + 2], axis=1) for sl in range(SGU_WIDTH // LANES)]


def _sgu_rows(u, v, gain, bias, pair_weights, b_mat):
    f32 = jnp.float32
    u = _gelu(u.astype(f32))
    v = _gelu(v.astype(f32))
    mu = jnp.mean(v, axis=-1, keepdims=True)
    vc = v - mu
    v = vc * lax.rsqrt(jnp.mean(vc * vc, axis=-1, keepdims=True) + LN_EPS)
    v = (v * gain + bias).astype(jnp.bfloat16)

    lane = lax.broadcasted_iota(jnp.int32, (SGU_CHUNK, LANES), 1)
    first_group = lane < (SGU_WIDTH // SGU_GROUPS)
    chunks = []
    for n in range(u.shape[0] // SGU_CHUNK):
        rows = slice(n * SGU_CHUNK, (n + 1) * SGU_CHUNK)
        slabs = []
        for sl in range(SGU_WIDTH // LANES):
            cols = slice(sl * LANES, (sl + 1) * LANES)
            vs = v[rows, cols]
            zero = jnp.zeros_like(vs)
            v_diag = jnp.concatenate([jnp.where(first_group, vs, zero), jnp.where(first_group, zero, vs)], axis=0)
            mixed = jnp.dot(pair_weights[sl], v_diag, preferred_element_type=f32) + b_mat[:, cols]
            slabs.append(u[rows, cols] * mixed)
        chunks.append(jnp.concatenate(slabs, axis=1))
    return jnp.concatenate(chunks, axis=0)


def _out_ffn_kernel(x_ref, attn_ref, sgu_ref, ga_ref, gs_ref, wo_ref, gpm_ref, gpf_ref,
                    wg_ref, wu_ref, wd_ref, gpo_ref, o_ref):
    f32 = jnp.float32
    bf16 = jnp.bfloat16
    sub = x_ref.shape[0] // FFN_SUBTILES
    row_slices = [slice(t * sub, (t + 1) * sub) for t in range(FFN_SUBTILES)]

    def out_proj(rows):
        a = _rms(attn_ref[rows, :].astype(f32), ga_ref[...]).astype(bf16)
        s = _rms(sgu_ref[rows, :].astype(f32), gs_ref[...]).astype(bf16)
        return (jnp.dot(a, wo_ref[0:ATTN_WIDTH, :], preferred_element_type=f32)
                + jnp.dot(s, wo_ref[ATTN_WIDTH:, :], preferred_element_type=f32))

    def ffn(rows, y):
        x1 = x_ref[rows, :] + _rms(y, gpm_ref[...])
        h = _rms(x1, gpf_ref[...]).astype(bf16)
        gate = jnp.dot(h, wg_ref[...], preferred_element_type=f32)
        up = jnp.dot(h, wu_ref[...], preferred_element_type=f32)
        act = (gate * jax.nn.sigmoid(gate) * up).astype(bf16)
        return x1, jnp.dot(act, wd_ref[...], preferred_element_type=f32)

    y_next = out_proj(row_slices[0])
    for t, rows in enumerate(row_slices):
        y = y_next
        if t + 1 < FFN_SUBTILES:
            y_next = out_proj(row_slices[t + 1])
        x1, f = ffn(rows, y)
        o_ref[rows, :] = x1 + _rms(f, gpo_ref[...])


def _out_ffn(x2, attn, sgu, ga, gs, wo, gpm, gpf, wg, wu, wd, gpo):
    n_tok = x2.shape[0]
    tm = FFN_TOKEN_TILE

    def const(shape):
        return pl.BlockSpec(shape, lambda i: (0,) * len(shape), pipeline_mode=pl.Buffered(1))

    return pl.pallas_call(
        _out_ffn_kernel,
        grid=(n_tok // tm,),
        in_specs=[
            pl.BlockSpec((tm, D_MODEL), lambda i: (i, 0)),
            pl.BlockSpec((tm, ATTN_WIDTH), lambda i: (i, 0)),
            pl.BlockSpec((tm, SGU_WIDTH), lambda i: (i, 0)),
            const((1, ATTN_WIDTH)),
            const((1, SGU_WIDTH)),
            const((D_MODEL, D_MODEL)),
            const((1, D_MODEL)),
            const((1, D_MODEL)),
            const((D_MODEL, D_FF)),
            const((D_MODEL, D_FF)),
            const((D_FF, D_MODEL)),
            const((1, D_MODEL)),
        ],
        out_specs=pl.BlockSpec((tm, D_MODEL), lambda i: (i, 0)),
        out_shape=jax.ShapeDtypeStruct((n_tok, D_MODEL), jnp.float32),
        compiler_params=pltpu.CompilerParams(
            dimension_semantics=("arbitrary",), vmem_limit_bytes=VMEM_LIMIT_BYTES),
        name="out_ffn",
    )(x2, attn, sgu, ga, gs, wo, gpm, gpf, wg, wu, wd, gpo)


def kernel(x, positions, pre_mix_norm, w_in, sgu_ln_gain, sgu_ln_bias, sgu_w_spatial, sgu_b_spatial, attn_out_norm, sgu_out_norm, w_out, post_mix_norm, pre_ffn_norm, w_gate, w_up, w_down, post_ffn_norm):
    batch, seq, _ = x.shape
    bf16 = jnp.bfloat16
    n_tok = batch * seq
    x2 = x.reshape(n_tok, D_MODEL)
    pos_row = positions.reshape(1, n_tok)

    depth = w_in.shape[0]
    for l in range(depth):
        b_mat = jnp.repeat(sgu_b_spatial[l].T, SGU_WIDTH // SGU_GROUPS, axis=1)
        qkv, sgu, (wo, wg, wu, wd) = _in_proj(
            x2, pos_row, pre_mix_norm[l][None, :], w_in[l],
            sgu_ln_gain[l][None, :], sgu_ln_bias[l][None, :], sgu_w_spatial[l], b_mat,
            later_weights=(w_out[l], w_gate[l], w_up[l], w_down[l]))
        attn = _attention(qkv, batch, seq)
        x2 = _out_ffn(x2, attn, sgu, attn_out_norm[l][None, :], sgu_out_norm[l][None, :],
                      wo, post_mix_norm[l][None, :], pre_ffn_norm[l][None, :], wg, wu, wd,
                      post_ffn_norm[l][None, :])
    return x2.reshape(batch, seq, D_MODEL)
```

```python
import functools

import jax
import jax.numpy as jnp
import numpy as np
from jax import lax
from jax.experimental import pallas as pl
from jax.experimental.pallas import tpu as pltpu

D_MODEL = 1024
HEAD_DIM = 64
ATTN_WIDTH = 512
SGU_WIDTH = 512
SGU_GROUPS = 8
SGU_CHUNK = 128
DILATIONS = (1, 4, 16)
WINDOW_STEPS = 128
BLOCK_Q = 128
DILATION_STEP = 4
ATTN_LOOKAHEAD = 3
SCORE_SCALE = float(np.float32(0.125 * np.log2(np.e)))
ROPE_THETA = 500000.0
ROT_DIM = 16
D_FF = 2816
RMS_EPS = 1e-6
LN_EPS = 1e-5
MASK_VALUE = float(np.finfo(np.float32).min)

LANES = 128
BF16_SUBLANE_TILE = 16
VMEM_LIMIT_BYTES = 56 * 1024 * 1024

IN_PROJ_TOKEN_TILE = 1024
IN_PROJ_SUBTILES = 4
FFN_TOKEN_TILE = 1024
FFN_SUBTILES = 4


def _rms(x, gain):
    return x * lax.rsqrt(jnp.mean(x * x, axis=-1, keepdims=True) + RMS_EPS) * gain


def _rotary_tables(pos_row, invf_col, spread):
    f32, bf16 = jnp.float32, jnp.bfloat16
    ang = invf_col * pos_row.astype(f32)
    trig = jnp.concatenate([jnp.cos(ang), jnp.sin(ang)], axis=0)
    hi = trig.astype(bf16)
    rest = trig - hi.astype(f32)
    mid = rest.astype(bf16)
    lo = (rest - mid.astype(f32)).astype(bf16)
    terms = jnp.concatenate([hi, mid, lo], axis=0)
    return lax.dot_general(terms, spread, (((0,), (0,)), ((), ())), preferred_element_type=f32)


def _in_proj_kernel(x_ref, pos_ref, gain_ref, invf_ref, spread_ref, w32_ref,
                    ln_gain_ref, ln_bias_ref, ws_ref, bs_ref, *rest):
    n_cast = len(rest) // 2 - 1
    cast_src, (qkv_ref, sgu_ref), cast_dst, w_ref = (
        rest[:n_cast], rest[n_cast:n_cast + 2], rest[n_cast + 2:2 * n_cast + 2], rest[-1])

    @pl.when(pl.program_id(0) == 0)
    def _():
        w_ref[...] = w32_ref[...].astype(w_ref.dtype)

    for src, dst in zip(cast_src, cast_dst):
        dst[...] = src[...].astype(dst.dtype)

    tables = _rotary_tables(pos_ref[...], invf_ref[...], spread_ref[...])
    pair_weights = _sgu_pair_weights(ws_ref)
    lane = lax.broadcasted_iota(jnp.int32, (1, LANES), 1)
    ones_off_rotary = jnp.where((lane % HEAD_DIM) < ROT_DIM, 0.0, 1.0)
    half = ROT_DIM // 2
    q_slabs = ATTN_WIDTH // LANES
    qk_width = 2 * ATTN_WIDTH
    uv_lo = 3 * ATTN_WIDTH

    sub = x_ref.shape[0] // IN_PROJ_SUBTILES
    row_slices = [slice(t * sub, (t + 1) * sub) for t in range(IN_PROJ_SUBTILES)]

    def project(rows):
        h = _rms(x_ref[rows, :], gain_ref[...]).astype(jnp.bfloat16)
        return jnp.dot(h, w_ref[...], preferred_element_type=jnp.float32)

    proj_next = project(row_slices[0])
    for t, rows in enumerate(row_slices):
        proj = proj_next
        if t + 1 < IN_PROJ_SUBTILES:
            proj_next = project(row_slices[t + 1])
        cos = tables[rows, :LANES] + ones_off_rotary
        sin_from_below = tables[rows, LANES:2 * LANES]
        sin_from_above = tables[rows, 2 * LANES:]
        for j in range(2 * q_slabs):
            slab = proj[:, j * LANES:(j + 1) * LANES]
            rot = (slab * cos
                   + pltpu.roll(slab, half, axis=1) * sin_from_below
                   + pltpu.roll(slab, LANES - half, axis=1) * sin_from_above)
            if j < q_slabs:
                rot = rot * SCORE_SCALE
            qkv_ref[rows, j * LANES:(j + 1) * LANES] = rot.astype(qkv_ref.dtype)
        qkv_ref[rows, qk_width:] = proj[:, qk_width:uv_lo].astype(qkv_ref.dtype)
        sgu = _sgu_rows(proj[:, uv_lo:uv_lo + SGU_WIDTH], proj[:, uv_lo + SGU_WIDTH:],
                        ln_gain_ref[...], ln_bias_ref[...], pair_weights, bs_ref[...])
        sgu_ref[rows, :] = sgu.astype(sgu_ref.dtype)


def _rotary_constants():
    n_freq = ROT_DIM // 2
    inv_freq = ROPE_THETA ** (-jnp.arange(0, ROT_DIM, 2, dtype=jnp.float32) / ROT_DIM)
    lane = np.arange(LANES)
    in_rot = (lane % HEAD_DIM) < ROT_DIM
    upper = (lane % ROT_DIM) >= n_freq
    hit = (lane[None, :] % n_freq) == np.arange(n_freq)[:, None]
    cos_rows = np.concatenate([hit & in_rot, np.zeros_like(hit), np.zeros_like(hit)], axis=1)
    sin_rows = np.concatenate([np.zeros(hit.shape), 1.0 * (hit & in_rot & upper),
                               -1.0 * (hit & in_rot & ~upper)], axis=1)
    one_term = np.concatenate([cos_rows.astype(np.float32), sin_rows.astype(np.float32)], axis=0)
    spread = np.concatenate([one_term] * 3, axis=0)
    return inv_freq.reshape(n_freq, 1), jnp.asarray(spread, dtype=jnp.bfloat16)


def _in_proj(x2, pos_row, gain, w_in, ln_gain, ln_bias, w_spatial, b_mat, later_weights):
    n_tok = x2.shape[0]
    tm = IN_PROJ_TOKEN_TILE
    n_steps = n_tok // tm
    width = w_in.shape[1]
    bf16 = jnp.bfloat16
    invf, spread = _rotary_constants()
    assert (tm // IN_PROJ_SUBTILES) % SGU_CHUNK == 0

    def const(shape, **kwargs):
        return pl.BlockSpec(shape, lambda i: (0,) * len(shape), **kwargs)

    def row_slab(w):
        rows = w.shape[0] // n_steps
        assert rows * n_steps == w.shape[0] and rows % BF16_SUBLANE_TILE == 0
        return pl.BlockSpec((rows, w.shape[1]), lambda i: (i, 0))

    slabs = [row_slab(w) for w in later_weights]
    outs = pl.pallas_call(
        _in_proj_kernel,
        grid=(n_steps,),
        in_specs=[
            pl.BlockSpec((tm, D_MODEL), lambda i: (i, 0)),
            pl.BlockSpec((1, tm), lambda i: (0, i)),
            const((1, D_MODEL)),
            const(invf.shape),
            const(spread.shape),
            const((D_MODEL, width), pipeline_mode=pl.Buffered(1)),
            const((1, SGU_WIDTH)),
            const((1, SGU_WIDTH)),
            const((SGU_GROUPS, SGU_CHUNK, SGU_CHUNK)),
            const((SGU_CHUNK, SGU_WIDTH)),
        ] + slabs,
        out_specs=[
            pl.BlockSpec((tm, 3 * ATTN_WIDTH), lambda i: (i, 0)),
            pl.BlockSpec((tm, SGU_WIDTH), lambda i: (i, 0)),
        ] + slabs,
        out_shape=[
            jax.ShapeDtypeStruct((n_tok, 3 * ATTN_WIDTH), bf16),
            jax.ShapeDtypeStruct((n_tok, SGU_WIDTH), bf16),
        ] + [jax.ShapeDtypeStruct(w.shape, bf16) for w in later_weights],
        scratch_shapes=[pltpu.VMEM((D_MODEL, width), bf16)],
        compiler_params=pltpu.CompilerParams(
            dimension_semantics=("arbitrary",), vmem_limit_bytes=VMEM_LIMIT_BYTES),
        name="in_proj",
    )(x2, pos_row, gain, invf, spread, w_in, ln_gain, ln_bias, w_spatial, b_mat, *later_weights)
    return outs[0], outs[1], outs[2:]


def _attn_kernel(q_ref, k_ref, v_ref, o_ref,
                 q32, k32, v32, q32g, k32g, v32g, qp1, kp1, vp1, qp2, kp2, vp2,
                 bias_ref, ot, mt, dt, out32, *, seq):
    n_blocks = seq // BLOCK_Q
    n_pat = len(DILATIONS)
    f32 = jnp.float32
    bf16 = jnp.bfloat16

    row = lax.broadcasted_iota(jnp.int32, (BLOCK_Q, 2 * BLOCK_Q), 0)
    col = lax.broadcasted_iota(jnp.int32, (BLOCK_Q, 2 * BLOCK_Q), 1)
    keep = (col >= row) & (col <= row + WINDOW_STEPS)
    bias_ref[...] = jnp.where(keep, 0.0, MASK_VALUE)

    lane = lax.broadcasted_iota(jnp.int32, (BLOCK_Q, LANES), 1)
    first_head = lane < HEAD_DIM

    def head_select(a, b):
        return jnp.where(first_head, a, b)

    def regroup(src, chunk, dst32, dst16):
        quarter = chunk // DILATION_STEP
        for c in range(seq // chunk):
            for r in range(DILATION_STEP):
                rows = src[pl.ds(c * chunk + r, quarter, stride=DILATION_STEP), :]
                lo = c * chunk + r * quarter
                if dst32 is not None:
                    dst32[lo:lo + quarter, :] = rows
                dst16[lo:lo + quarter, :] = rows.astype(bf16)

    sources = [(q_ref, k_ref, v_ref), (qp1, kp1, vp1), (qp2, kp2, vp2)]
    for x_ref, x32, x32g, xp1, xp2 in ((q_ref, q32, q32g, qp1, qp2), (k_ref, k32, k32g, kp1, kp2),
                                       (v_ref, v32, v32g, vp1, vp2)):
        x32[...] = x_ref[...].astype(f32)
        regroup(x32, seq, x32g, xp1)
        regroup(x32g, seq // DILATION_STEP, None, xp2)

    def level1_rows(level, i):
        if level < 2:
            return pl.ds(i * BLOCK_Q, BLOCK_Q)
        c, r = divmod(i, DILATION_STEP)
        return pl.ds(c * DILATION_STEP * BLOCK_Q + r, BLOCK_Q, stride=DILATION_STEP)

    def key_window(level, i):
        blocks_per_sub = n_blocks // DILATIONS[level]
        key_lo = i * BLOCK_Q if i % blocks_per_sub == 0 else (i - 1) * BLOCK_Q
        return key_lo, (i + 1) * BLOCK_Q - key_lo

    def probabilities(level, i):
        q_src, k_src, _ = sources[level]
        key_lo, n_keys = key_window(level, i)
        qb = q_src[i * BLOCK_Q:(i + 1) * BLOCK_Q, :]
        kw = k_src[key_lo:key_lo + n_keys, :]
        bias = bias_ref[:, 2 * BLOCK_Q - n_keys:]
        zero = jnp.zeros_like(qb)
        q2 = jnp.concatenate([jnp.where(first_head, qb, zero), jnp.where(first_head, zero, qb)], axis=0)
        s = lax.dot_general(q2, kw, (((1,), (1,)), ((), ())), preferred_element_type=f32)
        s = s + jnp.concatenate([bias, bias], axis=0)
        m = jnp.max(s, axis=-1, keepdims=True)
        return jnp.exp2(s - m).astype(bf16), m

    def weighted_values(level, i, e, m):
        key_lo, n_keys = key_window(level, i)
        vw = sources[level][2][key_lo:key_lo + n_keys, :]
        v_ext = jnp.concatenate([vw, jnp.ones((n_keys, LANES), bf16)], axis=1)
        o2 = jnp.dot(e, v_ext, preferred_element_type=f32)
        dst = level1_rows(level, i)
        m_wide = jnp.broadcast_to(m, (2 * BLOCK_Q, LANES))
        ot[level, dst, :] = head_select(o2[:BLOCK_Q, :LANES], o2[BLOCK_Q:, :LANES])
        dt[level, dst, :] = head_select(o2[:BLOCK_Q, LANES:], o2[BLOCK_Q:, LANES:])
        mt[level, dst, :] = head_select(m_wide[:BLOCK_Q], m_wide[BLOCK_Q:])

    quarter_blocks = n_blocks // DILATION_STEP

    def merge(b):
        r, j = divmod(b, quarter_blocks)
        rows0 = pl.ds(r + DILATION_STEP * BLOCK_Q * j, BLOCK_Q, stride=DILATION_STEP)
        rows1 = pl.ds(b * BLOCK_Q, BLOCK_Q)
        rows = [rows0] + [rows1] * (n_pat - 1)
        maxes = [mt[p, rows[p], :] for p in range(n_pat)]
        m_all = functools.reduce(jnp.maximum, maxes)
        num = jnp.zeros((BLOCK_Q, LANES), f32)
        den = jnp.zeros((BLOCK_Q, LANES), f32)
        for p in range(n_pat):
            w = jnp.exp2(maxes[p] - m_all)
            num = num + w * ot[p, rows[p], :]
            den = den + w * dt[p, rows[p], :]
        out32[rows0, :] = num / den

    order = [(0, i) for i in range(DILATION_STEP)]
    for i in range(n_blocks):
        order += [(0, i + DILATION_STEP)] * (i + DILATION_STEP < n_blocks) + [(2, i)]
    order += [(1, b) for b in range(n_blocks)]
    pending = {}
    for n in range(len(order) + ATTN_LOOKAHEAD):
        if n < len(order):
            pending[order[n]] = probabilities(*order[n])
        if n >= ATTN_LOOKAHEAD:
            level, i = order[n - ATTN_LOOKAHEAD]
            weighted_values(level, i, *pending.pop((level, i)))
            if level == 1:
                merge(i)
    o_ref[...] = out32[...].astype(o_ref.dtype)


def _attention(qkv, batch, seq):
    n_tok = qkv.shape[0]
    n_pairs = ATTN_WIDTH // LANES
    f32, bf16 = jnp.float32, jnp.bfloat16
    n_pat = len(DILATIONS)
    assert all(d == DILATION_STEP ** n for n, d in enumerate(DILATIONS)) and n_pat == 3
    assert seq % (DILATIONS[-1] * BLOCK_Q) == 0
    return pl.pallas_call(
        functools.partial(_attn_kernel, seq=seq),
        grid=(batch, n_pairs),
        in_specs=[
            pl.BlockSpec((seq, LANES), lambda b, h: (b, h)),
            pl.BlockSpec((seq, LANES), lambda b, h: (b, n_pairs + h)),
            pl.BlockSpec((seq, LANES), lambda b, h: (b, 2 * n_pairs + h)),
        ],
        out_specs=pl.BlockSpec((seq, LANES), lambda b, h: (b, h)),
        out_shape=jax.ShapeDtypeStruct((n_tok, ATTN_WIDTH), bf16),
        scratch_shapes=(
            [pltpu.VMEM((seq, LANES), f32)] * 6
            + [pltpu.VMEM((seq, LANES), bf16)] * 6
            + [pltpu.VMEM((BLOCK_Q, 2 * BLOCK_Q), f32)]
            + [pltpu.VMEM((n_pat, seq, LANES), f32)] * 3
            + [pltpu.VMEM((seq, LANES), f32)]
        ),
        compiler_params=pltpu.CompilerParams(
            dimension_semantics=("arbitrary", "arbitrary"), vmem_limit_bytes=VMEM_LIMIT_BYTES),
        name="attention",
    )(qkv, qkv, qkv)


def _gelu(x):
    return 0.5 * x * (1.0 + lax.erf(x * np.float32(1.0 / np.sqrt(2.0))))


def _sgu_pair_weights(w_ref):
    row = lax.broadcasted_iota(jnp.int32, (SGU_CHUNK, SGU_CHUNK), 0)
    col = lax.broadcasted_iota(jnp.int32, (SGU_CHUNK, SGU_CHUNK), 1)
    weights = [jnp.where(col <= row, w_ref[g], 0.0).astype(jnp.bfloat16) for g in range(SGU_GROUPS)]
    return [jnp.concatenate(weights[2 * sl:2 * sl + 2], axis=1) for sl in range(SGU_WIDTH // LANES)]


def _sgu_rows(u, v, gain, bias, pair_weights, b_mat):
    f32 = jnp.float32
    u = _gelu(u.astype(f32))
    v = _gelu(v.astype(f32))
    mu = jnp.mean(v, axis=-1, keepdims=True)
    vc = v - mu
    v = vc * lax.rsqrt(jnp.mean(vc * vc, axis=-1, keepdims=True) + LN_EPS)
    v = (v * gain + bias).astype(jnp.bfloat16)

    lane = lax.broadcasted_iota(jnp.int32, (SGU_CHUNK, LANES), 1)
    first_group = lane < (SGU_WIDTH // SGU_GROUPS)
    chunks = []
    for n in range(u.shape[0] // SGU_CHUNK):
        rows = slice(n * SGU_CHUNK, (n + 1) * SGU_CHUNK)
        slabs = []
        for sl in range(SGU_WIDTH // LANES):
            cols = slice(sl * LANES, (sl + 1) * LANES)
            vs = v[rows, cols]
            zero = jnp.zeros_like(vs)
            v_diag = jnp.concatenate([jnp.where(first_group, vs, zero), jnp.where(first_group, zero, vs)], axis=0)
            mixed = jnp.dot(pair_weights[sl], v_diag, preferred_element_type=f32) + b_mat[:, cols]
            slabs.append(u[rows, cols] * mixed)
        chunks.append(jnp.concatenate(slabs, axis=1))
    return jnp.concatenate(chunks, axis=0)


def _out_ffn_kernel(x_ref, attn_ref, sgu_ref, ga_ref, gs_ref, wo_ref, gpm_ref, gpf_ref,
                    wg_ref, wu_ref, wd_ref, gpo_ref, o_ref):
    f32 = jnp.float32
    bf16 = jnp.bfloat16
    sub = x_ref.shape[0] // FFN_SUBTILES
    row_slices = [slice(t * sub, (t + 1) * sub) for t in range(FFN_SUBTILES)]

    def out_proj(rows):
        a = _rms(attn_ref[rows, :].astype(f32), ga_ref[...]).astype(bf16)
        s = _rms(sgu_ref[rows, :].astype(f32), gs_ref[...]).astype(bf16)
        return (jnp.dot(a, wo_ref[0:ATTN_WIDTH, :], preferred_element_type=f32)
                + jnp.dot(s, wo_ref[ATTN_WIDTH:, :], preferred_element_type=f32))

    def ffn(rows, y):
        x1 = x_ref[rows, :] + _rms(y, gpm_ref[...])
        h = _rms(x1, gpf_ref[...]).astype(bf16)
        gate = jnp.dot(h, wg_ref[...], preferred_element_type=f32)
        up = jnp.dot(h, wu_ref[...], preferred_element_type=f32)
        act = (gate * jax.nn.sigmoid(gate) * up).astype(bf16)
        return x1, jnp.dot(act, wd_ref[...], preferred_element_type=f32)

    y_next = out_proj(row_slices[0])
    for t, rows in enumerate(row_slices):
        y = y_next
        if t + 1 < FFN_SUBTILES:
            y_next = out_proj(row_slices[t + 1])
        x1, f = ffn(rows, y)
        o_ref[rows, :] = x1 + _rms(f, gpo_ref[...])


def _out_ffn(x2, attn, sgu, ga, gs, wo, gpm, gpf, wg, wu, wd, gpo):
    n_tok = x2.shape[0]
    tm = FFN_TOKEN_TILE

    def const(shape):
        return pl.BlockSpec(shape, lambda i: (0,) * len(shape), pipeline_mode=pl.Buffered(1))

    return pl.pallas_call(
        _out_ffn_kernel,
        grid=(n_tok // tm,),
        in_specs=[
            pl.BlockSpec((tm, D_MODEL), lambda i: (i, 0)),
            pl.BlockSpec((tm, ATTN_WIDTH), lambda i: (i, 0)),
            pl.BlockSpec((tm, SGU_WIDTH), lambda i: (i, 0)),
            const((1, ATTN_WIDTH)),
            const((1, SGU_WIDTH)),
            const((D_MODEL, D_MODEL)),
            const((1, D_MODEL)),
            const((1, D_MODEL)),
            const((D_MODEL, D_FF)),
            const((D_MODEL, D_FF)),
            const((D_FF, D_MODEL)),
            const((1, D_MODEL)),
        ],
        out_specs=pl.BlockSpec((tm, D_MODEL), lambda i: (i, 0)),
        out_shape=jax.ShapeDtypeStruct((n_tok, D_MODEL), jnp.float32),
        compiler_params=pltpu.CompilerParams(
            dimension_semantics=("arbitrary",), vmem_limit_bytes=VMEM_LIMIT_BYTES),
        name="out_ffn",
    )(x2, attn, sgu, ga, gs, wo, gpm, gpf, wg, wu, wd, gpo)


def kernel(x, positions, pre_mix_norm, w_in, sgu_ln_gain, sgu_ln_bias, sgu_w_spatial, sgu_b_spatial, attn_out_norm, sgu_out_norm, w_out, post_mix_norm, pre_ffn_norm, w_gate, w_up, w_down, post_ffn_norm):
    batch, seq, _ = x.shape
    n_tok = batch * seq
    x2 = x.reshape(n_tok, D_MODEL)
    pos_row = positions.reshape(1, n_tok)

    depth = w_in.shape[0]
    for l in range(depth):
        b_mat = jnp.repeat(sgu_b_spatial[l].T, SGU_WIDTH // SGU_GROUPS, axis=1)
        qkv, sgu, (wo, wg, wu, wd) = _in_proj(
            x2, pos_row, pre_mix_norm[l][None, :], w_in[l],
            sgu_ln_gain[l][None, :], sgu_ln_bias[l][None, :], sgu_w_spatial[l], b_mat,
            later_weights=(w_out[l], w_gate[l], w_up[l], w_down[l]))
        attn = _attention(qkv, batch, seq)
        x2 = _out_ffn(x2, attn, sgu, attn_out_norm[l][None, :], sgu_out_norm[l][None, :],
                      wo, post_mix_norm[l][None, :], pre_ffn_norm[l][None, :], wg, wu, wd,
                      post_ffn_norm[l][None, :])
    return x2.reshape(batch, seq, D_MODEL)
```

```python
import functools

import jax
import jax.numpy as jnp
import numpy as np
from jax import lax
from jax.experimental import pallas as pl
from jax.experimental.pallas import tpu as pltpu

D_MODEL = 1024
HEAD_DIM = 64
ATTN_WIDTH = 512
SGU_WIDTH = 512
SGU_GROUPS = 8
SGU_CHUNK = 128
DILATIONS = (1, 4, 16)
WINDOW_STEPS = 128
BLOCK_Q = 128
DILATION_STEP = 4
ATTN_LOOKAHEAD = 2
SCORE_SCALE = float(np.float32(0.125 * np.log2(np.e)))
ROPE_THETA = 500000.0
ROT_DIM = 16
D_FF = 2816
RMS_EPS = 1e-6
LN_EPS = 1e-5
MASK_VALUE = float(np.finfo(np.float32).min)

LANES = 128
BF16_SUBLANE_TILE = 16
VMEM_LIMIT_BYTES = 56 * 1024 * 1024

IN_PROJ_TOKEN_TILE = 1024
IN_PROJ_SUBTILES = 4
FFN_TOKEN_TILE = 1024
FFN_SUBTILES = 4


def _rms(x, gain):
    return x * lax.rsqrt(jnp.mean(x * x, axis=-1, keepdims=True) + RMS_EPS) * gain


def _rotary_tables(pos_row, invf_col, spread):
    f32, bf16 = jnp.float32, jnp.bfloat16
    ang = invf_col * pos_row.astype(f32)
    trig = jnp.concatenate([jnp.cos(ang), jnp.sin(ang)], axis=0)
    hi = trig.astype(bf16)
    rest = trig - hi.astype(f32)
    mid = rest.astype(bf16)
    lo = (rest - mid.astype(f32)).astype(bf16)
    terms = jnp.concatenate([hi, mid, lo], axis=0)
    return lax.dot_general(terms, spread, (((0,), (0,)), ((), ())), preferred_element_type=f32)


def _in_proj_kernel(x_ref, pos_ref, gain_ref, invf_ref, spread_ref, w32_ref,
                    ln_gain_ref, ln_bias_ref, ws_ref, bs_ref, *rest):
    n_cast = len(rest) // 2 - 1
    cast_src, (qkv_ref, sgu_ref), cast_dst, w_ref = (
        rest[:n_cast], rest[n_cast:n_cast + 2], rest[n_cast + 2:2 * n_cast + 2], rest[-1])

    @pl.when(pl.program_id(0) == 0)
    def _():
        w_ref[...] = w32_ref[...].astype(w_ref.dtype)

    for src, dst in zip(cast_src, cast_dst):
        dst[...] = src[...].astype(dst.dtype)

    tables = _rotary_tables(pos_ref[...], invf_ref[...], spread_ref[...])
    pair_weights = _sgu_pair_weights(ws_ref)
    lane = lax.broadcasted_iota(jnp.int32, (1, LANES), 1)
    ones_off_rotary = jnp.where((lane % HEAD_DIM) < ROT_DIM, 0.0, 1.0)
    half = ROT_DIM // 2
    q_slabs = ATTN_WIDTH // LANES
    qk_width = 2 * ATTN_WIDTH
    uv_lo = 3 * ATTN_WIDTH

    sub = x_ref.shape[0] // IN_PROJ_SUBTILES
    row_slices = [slice(t * sub, (t + 1) * sub) for t in range(IN_PROJ_SUBTILES)]

    def project(rows):
        h = _rms(x_ref[rows, :], gain_ref[...]).astype(jnp.bfloat16)
        return jnp.dot(h, w_ref[...], preferred_element_type=jnp.float32)

    proj_next = project(row_slices[0])
    for t, rows in enumerate(row_slices):
        proj = proj_next
        if t + 1 < IN_PROJ_SUBTILES:
            proj_next = project(row_slices[t + 1])
        cos = tables[rows, :LANES] + ones_off_rotary
        sin_from_below = tables[rows, LANES:2 * LANES]
        sin_from_above = tables[rows, 2 * LANES:]
        for j in range(2 * q_slabs):
            slab = proj[:, j * LANES:(j + 1) * LANES]
            rot = (slab * cos
                   + pltpu.roll(slab, half, axis=1) * sin_from_below
                   + pltpu.roll(slab, LANES - half, axis=1) * sin_from_above)
            if j < q_slabs:
                rot = rot * SCORE_SCALE
            qkv_ref[rows, j * LANES:(j + 1) * LANES] = rot.astype(qkv_ref.dtype)
        qkv_ref[rows, qk_width:] = proj[:, qk_width:uv_lo].astype(qkv_ref.dtype)
        sgu = _sgu_rows(proj[:, uv_lo:uv_lo + SGU_WIDTH], proj[:, uv_lo + SGU_WIDTH:],
                        ln_gain_ref[...], ln_bias_ref[...], pair_weights, bs_ref[...])
        sgu_ref[rows, :] = sgu.astype(sgu_ref.dtype)


def _rotary_constants():
    n_freq = ROT_DIM // 2
    inv_freq = ROPE_THETA ** (-jnp.arange(0, ROT_DIM, 2, dtype=jnp.float32) / ROT_DIM)
    lane = np.arange(LANES)
    in_rot = (lane % HEAD_DIM) < ROT_DIM
    upper = (lane % ROT_DIM) >= n_freq
    hit = (lane[None, :] % n_freq) == np.arange(n_freq)[:, None]
    cos_rows = np.concatenate([hit & in_rot, np.zeros_like(hit), np.zeros_like(hit)], axis=1)
    sin_rows = np.concatenate([np.zeros(hit.shape), 1.0 * (hit & in_rot & upper),
                               -1.0 * (hit & in_rot & ~upper)], axis=1)
    one_term = np.concatenate([cos_rows.astype(np.float32), sin_rows.astype(np.float32)], axis=0)
    spread = np.concatenate([one_term] * 3, axis=0)
    return inv_freq.reshape(n_freq, 1), jnp.asarray(spread, dtype=jnp.bfloat16)


def _in_proj(x2, pos_row, gain, w_in, ln_gain, ln_bias, w_spatial, b_mat, later_weights):
    n_tok = x2.shape[0]
    tm = IN_PROJ_TOKEN_TILE
    n_steps = n_tok // tm
    width = w_in.shape[1]
    bf16 = jnp.bfloat16
    invf, spread = _rotary_constants()
    assert (tm // IN_PROJ_SUBTILES) % SGU_CHUNK == 0

    def const(shape, **kwargs):
        return pl.BlockSpec(shape, lambda i: (0,) * len(shape), **kwargs)

    def row_slab(w):
        rows = w.shape[0] // n_steps
        assert rows * n_steps == w.shape[0] and rows % BF16_SUBLANE_TILE == 0
        return pl.BlockSpec((rows, w.shape[1]), lambda i: (i, 0))

    slabs = [row_slab(w) for w in later_weights]
    outs = pl.pallas_call(
        _in_proj_kernel,
        grid=(n_steps,),
        in_specs=[
            pl.BlockSpec((tm, D_MODEL), lambda i: (i, 0)),
            pl.BlockSpec((1, tm), lambda i: (0, i)),
            const((1, D_MODEL)),
            const(invf.shape),
            const(spread.shape),
            const((D_MODEL, width), pipeline_mode=pl.Buffered(1)),
            const((1, SGU_WIDTH)),
            const((1, SGU_WIDTH)),
            const((SGU_GROUPS, SGU_CHUNK, SGU_CHUNK)),
            const((SGU_CHUNK, SGU_WIDTH)),
        ] + slabs,
        out_specs=[
            pl.BlockSpec((tm, 3 * ATTN_WIDTH), lambda i: (i, 0)),
            pl.BlockSpec((tm, SGU_WIDTH), lambda i: (i, 0)),
        ] + slabs,
        out_shape=[
            jax.ShapeDtypeStruct((n_tok, 3 * ATTN_WIDTH), bf16),
            jax.ShapeDtypeStruct((n_tok, SGU_WIDTH), bf16),
        ] + [jax.ShapeDtypeStruct(w.shape, bf16) for w in later_weights],
        scratch_shapes=[pltpu.VMEM((D_MODEL, width), bf16)],
        compiler_params=pltpu.CompilerParams(
            dimension_semantics=("arbitrary",), vmem_limit_bytes=VMEM_LIMIT_BYTES),
        name="in_proj",
    )(x2, pos_row, gain, invf, spread, w_in, ln_gain, ln_bias, w_spatial, b_mat, *later_weights)
    return outs[0], outs[1], outs[2:]


def _attn_kernel(q_ref, k_ref, v_ref, o_ref,
                 q32, k32, v32, q32g, k32g, v32g, qp1, kp1, vp1, qp2, kp2, vp2,
                 bias_ref, ot, mt, dt, out32, *, seq):
    n_blocks = seq // BLOCK_Q
    n_pat = len(DILATIONS)
    f32 = jnp.float32
    bf16 = jnp.bfloat16

    row = lax.broadcasted_iota(jnp.int32, (BLOCK_Q, 2 * BLOCK_Q), 0)
    col = lax.broadcasted_iota(jnp.int32, (BLOCK_Q, 2 * BLOCK_Q), 1)
    keep = (col >= row) & (col <= row + WINDOW_STEPS)
    bias_ref[...] = jnp.where(keep, 0.0, MASK_VALUE)

    lane = lax.broadcasted_iota(jnp.int32, (BLOCK_Q, LANES), 1)
    first_head = lane < HEAD_DIM

    def head_select(a, b):
        return jnp.where(first_head, a, b)

    def regroup(src, chunk, dst32, dst16):
        quarter = chunk // DILATION_STEP
        for c in range(seq // chunk):
            for r in range(DILATION_STEP):
                rows = src[pl.ds(c * chunk + r, quarter, stride=DILATION_STEP), :]
                lo = c * chunk + r * quarter
                if dst32 is not None:
                    dst32[lo:lo + quarter, :] = rows
                dst16[lo:lo + quarter, :] = rows.astype(bf16)

    sources = [(q_ref, k_ref, v_ref), (qp1, kp1, vp1), (qp2, kp2, vp2)]
    for x_ref, x32, x32g, xp1, xp2 in ((q_ref, q32, q32g, qp1, qp2), (k_ref, k32, k32g, kp1, kp2),
                                       (v_ref, v32, v32g, vp1, vp2)):
        x32[...] = x_ref[...].astype(f32)
        regroup(x32, seq, x32g, xp1)
        regroup(x32g, seq // DILATION_STEP, None, xp2)

    def level1_rows(level, i):
        if level < 2:
            return pl.ds(i * BLOCK_Q, BLOCK_Q)
        c, r = divmod(i, DILATION_STEP)
        return pl.ds(c * DILATION_STEP * BLOCK_Q + r, BLOCK_Q, stride=DILATION_STEP)

    def key_window(level, i):
        blocks_per_sub = n_blocks // DILATIONS[level]
        key_lo = i * BLOCK_Q if i % blocks_per_sub == 0 else (i - 1) * BLOCK_Q
        return key_lo, (i + 1) * BLOCK_Q - key_lo

    def probabilities(level, i):
        q_src, k_src, _ = sources[level]
        key_lo, n_keys = key_window(level, i)
        qb = q_src[i * BLOCK_Q:(i + 1) * BLOCK_Q, :]
        kw = k_src[key_lo:key_lo + n_keys, :]
        bias = bias_ref[:, 2 * BLOCK_Q - n_keys:]
        zero = jnp.zeros_like(qb)
        q2 = jnp.concatenate([jnp.where(first_head, qb, zero), jnp.where(first_head, zero, qb)], axis=0)
        s = lax.dot_general(q2, kw, (((1,), (1,)), ((), ())), preferred_element_type=f32)
        s = s + jnp.concatenate([bias, bias], axis=0)
        m = jnp.max(s, axis=-1, keepdims=True)
        return jnp.exp2(s - m).astype(bf16), m

    def weighted_values(level, i, e, m):
        key_lo, n_keys = key_window(level, i)
        vw = sources[level][2][key_lo:key_lo + n_keys, :]
        v_ext = jnp.concatenate([vw, jnp.ones((n_keys, LANES), bf16)], axis=1)
        o2 = jnp.dot(e, v_ext, preferred_element_type=f32)
        dst = level1_rows(level, i)
        m_wide = jnp.broadcast_to(m, (2 * BLOCK_Q, LANES))
        ot[level, dst, :] = head_select(o2[:BLOCK_Q, :LANES], o2[BLOCK_Q:, :LANES])
        dt[level, dst, :] = head_select(o2[:BLOCK_Q, LANES:], o2[BLOCK_Q:, LANES:])
        mt[level, dst, :] = head_select(m_wide[:BLOCK_Q], m_wide[BLOCK_Q:])

    quarter_blocks = n_blocks // DILATION_STEP

    def merge(b):
        r, j = divmod(b, quarter_blocks)
        rows0 = pl.ds(r + DILATION_STEP * BLOCK_Q * j, BLOCK_Q, stride=DILATION_STEP)
        rows1 = pl.ds(b * BLOCK_Q, BLOCK_Q)
        rows = [rows0] + [rows1] * (n_pat - 1)
        maxes = [mt[p, rows[p], :] for p in range(n_pat)]
        m_all = functools.reduce(jnp.maximum, maxes)
        num = jnp.zeros((BLOCK_Q, LANES), f32)
        den = jnp.zeros((BLOCK_Q, LANES), f32)
        for p in range(n_pat):
            w = jnp.exp2(maxes[p] - m_all)
            num = num + w * ot[p, rows[p], :]
            den = den + w * dt[p, rows[p], :]
        out32[rows0, :] = num / den

    order = [(0, i) for i in range(DILATION_STEP)]
    for i in range(n_blocks):
        order += [(0, i + DILATION_STEP)] * (i + DILATION_STEP < n_blocks) + [(2, i)]
    order += [(1, b) for b in range(n_blocks)]
    pending = {}
    for n in range(len(order) + ATTN_LOOKAHEAD):
        if n < len(order):
            pending[order[n]] = probabilities(*order[n])
        if n >= ATTN_LOOKAHEAD:
            level, i = order[n - ATTN_LOOKAHEAD]
            weighted_values(level, i, *pending.pop((level, i)))
            if level == 1:
                merge(i)
    o_ref[...] = out32[...].astype(o_ref.dtype)


def _attention(qkv, batch, seq):
    n_tok = qkv.shape[0]
    n_pairs = ATTN_WIDTH // LANES
    f32, bf16 = jnp.float32, jnp.bfloat16
    n_pat = len(DILATIONS)
    assert all(d == DILATION_STEP ** n for n, d in enumerate(DILATIONS)) and n_pat == 3
    assert seq % (DILATIONS[-1] * BLOCK_Q) == 0
    return pl.pallas_call(
        functools.partial(_attn_kernel, seq=seq),
        grid=(batch, n_pairs),
        in_specs=[
            pl.BlockSpec((seq, LANES), lambda b, h: (b, h)),
            pl.BlockSpec((seq, LANES), lambda b, h: (b, n_pairs + h)),
            pl.BlockSpec((seq, LANES), lambda b, h: (b, 2 * n_pairs + h)),
        ],
        out_specs=pl.BlockSpec((seq, LANES), lambda b, h: (b, h)),
        out_shape=jax.ShapeDtypeStruct((n_tok, ATTN_WIDTH), bf16),
        scratch_shapes=(
            [pltpu.VMEM((seq, LANES), f32)] * 6
            + [pltpu.VMEM((seq, LANES), bf16)] * 6
            + [pltpu.VMEM((BLOCK_Q, 2 * BLOCK_Q), f32)]
            + [pltpu.VMEM((n_pat, seq, LANES), f32)] * 3
            + [pltpu.VMEM((seq, LANES), f32)]
        ),
        compiler_params=pltpu.CompilerParams(
            dimension_semantics=("arbitrary", "arbitrary"), vmem_limit_bytes=VMEM_LIMIT_BYTES),
        name="attention",
    )(qkv, qkv, qkv)


def _gelu(x):
    return 0.5 * x * (1.0 + lax.erf(x * np.float32(1.0 / np.sqrt(2.0))))


def _sgu_pair_weights(w_ref):
    row = lax.broadcasted_iota(jnp.int32, (SGU_CHUNK, SGU_CHUNK), 0)
    col = lax.broadcasted_iota(jnp.int32, (SGU_CHUNK, SGU_CHUNK), 1)
    weights = [jnp.where(col <= row, w_ref[g], 0.0).astype(jnp.bfloat16) for g in range(SGU_GROUPS)]
    return [jnp.concatenate(weights[2 * sl:2 * sl + 2], axis=1) for sl in range(SGU_WIDTH // LANES)]


def _sgu_rows(u, v, gain, bias, pair_weights, b_mat):
    f32 = jnp.float32
    u = _gelu(u.astype(f32))
    v = _gelu(v.astype(f32))
    mu = jnp.mean(v, axis=-1, keepdims=True)
    vc = v - mu
    v = vc * lax.rsqrt(jnp.mean(vc * vc, axis=-1, keepdims=True) + LN_EPS)
    v = (v * gain + bias).astype(jnp.bfloat16)

    lane = lax.broadcasted_iota(jnp.int32, (SGU_CHUNK, LANES), 1)
    first_group = lane < (SGU_WIDTH // SGU_GROUPS)
    chunks = []
    for n in range(u.shape[0] // SGU_CHUNK):
        rows = slice(n * SGU_CHUNK, (n + 1) * SGU_CHUNK)
        slabs = []
        for sl in range(SGU_WIDTH // LANES):
            cols = slice(sl * LANES, (sl + 1) * LANES)
            vs = v[rows, cols]
            zero = jnp.zeros_like(vs)
            v_diag = jnp.concatenate([jnp.where(first_group, vs, zero), jnp.where(first_group, zero, vs)], axis=0)
            mixed = jnp.dot(pair_weights[sl], v_diag, preferred_element_type=f32) + b_mat[:, cols]
            slabs.append(u[rows, cols] * mixed)
        chunks.append(jnp.concatenate(slabs, axis=1))
    return jnp.concatenate(chunks, axis=0)


def _out_ffn_kernel(x_ref, attn_ref, sgu_ref, ga_ref, gs_ref, wo_ref, gpm_ref, gpf_ref,
                    wg_ref, wu_ref, wd_ref, gpo_ref, o_ref):
    f32 = jnp.float32
    bf16 = jnp.bfloat16
    sub = x_ref.shape[0] // FFN_SUBTILES
    row_slices = [slice(t * sub, (t + 1) * sub) for t in range(FFN_SUBTILES)]

    def out_proj(rows):
        a = _rms(attn_ref[rows, :].astype(f32), ga_ref[...]).astype(bf16)
        s = _rms(sgu_ref[rows, :].astype(f32), gs_ref[...]).astype(bf16)
        return (jnp.dot(a, wo_ref[0:ATTN_WIDTH, :], preferred_element_type=f32)
                + jnp.dot(s, wo_ref[ATTN_WIDTH:, :], preferred_element_type=f32))

    def ffn(rows, y):
        x1 = x_ref[rows, :] + _rms(y, gpm_ref[...])
        h = _rms(x1, gpf_ref[...]).astype(bf16)
        gate = jnp.dot(h, wg_ref[...], preferred_element_type=f32)
        up = jnp.dot(h, wu_ref[...], preferred_element_type=f32)
        act = (gate * jax.nn.sigmoid(gate) * up).astype(bf16)
        return x1, jnp.dot(act, wd_ref[...], preferred_element_type=f32)

    y_next = out_proj(row_slices[0])
    for t, rows in enumerate(row_slices):
        y = y_next
        if t + 1 < FFN_SUBTILES:
            y_next = out_proj(row_slices[t + 1])
        x1, f = ffn(rows, y)
        o_ref[rows, :] = x1 + _rms(f, gpo_ref[...])


def _out_ffn(x2, attn, sgu, ga, gs, wo, gpm, gpf, wg, wu, wd, gpo):
    n_tok = x2.shape[0]
    tm = FFN_TOKEN_TILE

    def const(shape):
        return pl.BlockSpec(shape, lambda i: (0,) * len(shape), pipeline_mode=pl.Buffered(1))

    return pl.pallas_call(
        _out_ffn_kernel,
        grid=(n_tok // tm,),
        in_specs=[
            pl.BlockSpec((tm, D_MODEL), lambda i: (i, 0)),
            pl.BlockSpec((tm, ATTN_WIDTH), lambda i: (i, 0)),
            pl.BlockSpec((tm, SGU_WIDTH), lambda i: (i, 0)),
            const((1, ATTN_WIDTH)),
            const((1, SGU_WIDTH)),
            const((D_MODEL, D_MODEL)),
            const((1, D_MODEL)),
            const((1, D_MODEL)),
            const((D_MODEL, D_FF)),
            const((D_MODEL, D_FF)),
            const((D_FF, D_MODEL)),
            const((1, D_MODEL)),
        ],
        out_specs=pl.BlockSpec((tm, D_MODEL), lambda i: (i, 0)),
        out_shape=jax.ShapeDtypeStruct((n_tok, D_MODEL), jnp.float32),
        compiler_params=pltpu.CompilerParams(
            dimension_semantics=("arbitrary",), vmem_limit_bytes=VMEM_LIMIT_BYTES),
        name="out_ffn",
    )(x2, attn, sgu, ga, gs, wo, gpm, gpf, wg, wu, wd, gpo)


def kernel(x, positions, pre_mix_norm, w_in, sgu_ln_gain, sgu_ln_bias, sgu_w_spatial, sgu_b_spatial, attn_out_norm, sgu_out_norm, w_out, post_mix_norm, pre_ffn_norm, w_gate, w_up, w_down, post_ffn_norm):
    batch, seq, _ = x.shape
    n_tok = batch * seq
    x2 = x.reshape(n_tok, D_MODEL)
    pos_row = positions.reshape(1, n_tok)

    depth = w_in.shape[0]
    for l in range(depth):
        b_mat = jnp.repeat(sgu_b_spatial[l].T, SGU_WIDTH // SGU_GROUPS, axis=1)
        qkv, sgu, (wo, wg, wu, wd) = _in_proj(
            x2, pos_row, pre_mix_norm[l][None, :], w_in[l],
            sgu_ln_gain[l][None, :], sgu_ln_bias[l][None, :], sgu_w_spatial[l], b_mat,
            later_weights=(w_out[l], w_gate[l], w_up[l], w_down[l]))
        attn = _attention(qkv, batch, seq)
        x2 = _out_ffn(x2, attn, sgu, attn_out_norm[l][None, :], sgu_out_norm[l][None, :],
                      wo, post_mix_norm[l][None, :], pre_ffn_norm[l][None, :], wg, wu, wd,
                      post_ffn_norm[l][None, :])
    return x2.reshape(batch, seq, D_MODEL)
```

```python
import functools

import jax
import jax.numpy as jnp
import numpy as np
from jax import lax
from jax.experimental import pallas as pl
from jax.experimental.pallas import tpu as pltpu

D_MODEL = 1024
HEAD_DIM = 64
ATTN_WIDTH = 512
SGU_WIDTH = 512
SGU_GROUPS = 8
SGU_CHUNK = 128
DILATIONS = (1, 4, 16)
WINDOW_STEPS = 128
BLOCK_Q = 128
DILATION_STEP = 4
ATTN_LOOKAHEAD = 3
SCORE_SCALE = float(np.float32(0.125 * np.log2(np.e)))
ROPE_THETA = 500000.0
ROT_DIM = 16
D_FF = 2816
RMS_EPS = 1e-6
LN_EPS = 1e-5
MASK_VALUE = float(np.finfo(np.float32).min)

LANES = 128
BF16_SUBLANE_TILE = 16
VMEM_LIMIT_BYTES = 56 * 1024 * 1024

IN_PROJ_TOKEN_TILE = 1024
IN_PROJ_SUBTILES = 4
FFN_TOKEN_TILE = 1024
FFN_SUBTILES = 4


def _rms(x, gain):
    return x * lax.rsqrt(jnp.mean(x * x, axis=-1, keepdims=True) + RMS_EPS) * gain


def _rotary_tables(pos_row, invf_col, spread):
    f32, bf16 = jnp.float32, jnp.bfloat16
    ang = invf_col * pos_row.astype(f32)
    trig = jnp.concatenate([jnp.cos(ang), jnp.sin(ang)], axis=0)
    hi = trig.astype(bf16)
    rest = trig - hi.astype(f32)
    mid = rest.astype(bf16)
    lo = (rest - mid.astype(f32)).astype(bf16)
    terms = jnp.concatenate([hi, mid, lo], axis=0)
    return lax.dot_general(terms, spread, (((0,), (0,)), ((), ())), preferred_element_type=f32)


def _in_proj_kernel(x_ref, pos_ref, gain_ref, invf_ref, spread_ref, w32_ref,
                    ln_gain_ref, ln_bias_ref, ws_ref, bs_ref, *rest):
    n_cast = len(rest) // 2 - 1
    cast_src, (qkv_ref, sgu_ref), cast_dst, w_ref = (
        rest[:n_cast], rest[n_cast:n_cast + 2], rest[n_cast + 2:2 * n_cast + 2], rest[-1])

    @pl.when(pl.program_id(0) == 0)
    def _():
        w_ref[...] = w32_ref[...].astype(w_ref.dtype)

    for src, dst in zip(cast_src, cast_dst):
        dst[...] = src[...].astype(dst.dtype)

    tables = _rotary_tables(pos_ref[...], invf_ref[...], spread_ref[...])
    pair_weights = _sgu_pair_weights(ws_ref)
    lane = lax.broadcasted_iota(jnp.int32, (1, LANES), 1)
    ones_off_rotary = jnp.where((lane % HEAD_DIM) < ROT_DIM, 0.0, 1.0)
    half = ROT_DIM // 2
    q_slabs = ATTN_WIDTH // LANES
    qk_width = 2 * ATTN_WIDTH
    uv_lo = 3 * ATTN_WIDTH

    sub = x_ref.shape[0] // IN_PROJ_SUBTILES
    row_slices = [slice(t * sub, (t + 1) * sub) for t in range(IN_PROJ_SUBTILES)]

    def project(rows):
        h = _rms(x_ref[rows, :], gain_ref[...]).astype(jnp.bfloat16)
        return jnp.dot(h, w_ref[...], preferred_element_type=jnp.float32)

    proj_next = project(row_slices[0])
    for t, rows in enumerate(row_slices):
        proj = proj_next
        if t + 1 < IN_PROJ_SUBTILES:
            proj_next = project(row_slices[t + 1])
        cos = tables[rows, :LANES] + ones_off_rotary
        sin_from_below = tables[rows, LANES:2 * LANES]
        sin_from_above = tables[rows, 2 * LANES:]
        for j in range(2 * q_slabs):
            slab = proj[:, j * LANES:(j + 1) * LANES]
            rot = (slab * cos
                   + pltpu.roll(slab, half, axis=1) * sin_from_below
                   + pltpu.roll(slab, LANES - half, axis=1) * sin_from_above)
            if j < q_slabs:
                rot = rot * SCORE_SCALE
            qkv_ref[rows, j * LANES:(j + 1) * LANES] = rot.astype(qkv_ref.dtype)
        qkv_ref[rows, qk_width:] = proj[:, qk_width:uv_lo].astype(qkv_ref.dtype)
        sgu = _sgu_rows(proj[:, uv_lo:uv_lo + SGU_WIDTH], proj[:, uv_lo + SGU_WIDTH:],
                        ln_gain_ref[...], ln_bias_ref[...], pair_weights, bs_ref[...])
        sgu_ref[rows, :] = sgu.astype(sgu_ref.dtype)


def _rotary_constants():
    n_freq = ROT_DIM // 2
    inv_freq = ROPE_THETA ** (-jnp.arange(0, ROT_DIM, 2, dtype=jnp.float32) / ROT_DIM)
    lane = np.arange(LANES)
    in_rot = (lane % HEAD_DIM) < ROT_DIM
    upper = (lane % ROT_DIM) >= n_freq
    hit = (lane[None, :] % n_freq) == np.arange(n_freq)[:, None]
    cos_rows = np.concatenate([hit & in_rot, np.zeros_like(hit), np.zeros_like(hit)], axis=1)
    sin_rows = np.concatenate([np.zeros(hit.shape), 1.0 * (hit & in_rot & upper),
                               -1.0 * (hit & in_rot & ~upper)], axis=1)
    one_term = np.concatenate([cos_rows.astype(np.float32), sin_rows.astype(np.float32)], axis=0)
    spread = np.concatenate([one_term] * 3, axis=0)
    return inv_freq.reshape(n_freq, 1), jnp.asarray(spread, dtype=jnp.bfloat16)


def _in_proj(x2, pos_row, gain, w_in, ln_gain, ln_bias, w_spatial, b_mat, later_weights):
    n_tok = x2.shape[0]
    tm = IN_PROJ_TOKEN_TILE
    n_steps = n_tok // tm
    width = w_in.shape[1]
    bf16 = jnp.bfloat16
    invf, spread = _rotary_constants()
    assert (tm // IN_PROJ_SUBTILES) % SGU_CHUNK == 0

    def const(shape, **kwargs):
        return pl.BlockSpec(shape, lambda i: (0,) * len(shape), **kwargs)

    def row_slab(w):
        rows = w.shape[0] // n_steps
        assert rows * n_steps == w.shape[0] and rows % BF16_SUBLANE_TILE == 0
        return pl.BlockSpec((rows, w.shape[1]), lambda i: (i, 0))

    slabs = [row_slab(w) for w in later_weights]
    outs = pl.pallas_call(
        _in_proj_kernel,
        grid=(n_steps,),
        in_specs=[
            pl.BlockSpec((tm, D_MODEL), lambda i: (i, 0)),
            pl.BlockSpec((1, tm), lambda i: (0, i)),
            const((1, D_MODEL)),
            const(invf.shape),
            const(spread.shape),
            const((D_MODEL, width), pipeline_mode=pl.Buffered(1)),
            const((1, SGU_WIDTH)),
            const((1, SGU_WIDTH)),
            const((SGU_GROUPS, SGU_CHUNK, SGU_CHUNK)),
            const((SGU_CHUNK, SGU_WIDTH)),
        ] + slabs,
        out_specs=[
            pl.BlockSpec((tm, 3 * ATTN_WIDTH), lambda i: (i, 0)),
            pl.BlockSpec((tm, SGU_WIDTH), lambda i: (i, 0)),
        ] + slabs,
        out_shape=[
            jax.ShapeDtypeStruct((n_tok, 3 * ATTN_WIDTH), bf16),
            jax.ShapeDtypeStruct((n_tok, SGU_WIDTH), bf16),
        ] + [jax.ShapeDtypeStruct(w.shape, bf16) for w in later_weights],
        scratch_shapes=[pltpu.VMEM((D_MODEL, width), bf16)],
        compiler_params=pltpu.CompilerParams(
            dimension_semantics=("arbitrary",), vmem_limit_bytes=VMEM_LIMIT_BYTES),
        name="in_proj",
    )(x2, pos_row, gain, invf, spread, w_in, ln_gain, ln_bias, w_spatial, b_mat, *later_weights)
    return outs[0], outs[1], outs[2:]


def _attn_kernel(q_ref, k_ref, v_ref, o_ref,
                 q32, k32, v32, q32g, k32g, v32g, qp1, kp1, vp1, qp2, kp2, vp2,
                 bias_ref, ot, mt, dt, out32, *, seq):
    n_blocks = seq // BLOCK_Q
    n_pat = len(DILATIONS)
    f32 = jnp.float32
    bf16 = jnp.bfloat16

    row = lax.broadcasted_iota(jnp.int32, (BLOCK_Q, 2 * BLOCK_Q), 0)
    col = lax.broadcasted_iota(jnp.int32, (BLOCK_Q, 2 * BLOCK_Q), 1)
    keep = (col >= row) & (col <= row + WINDOW_STEPS)
    bias_ref[...] = jnp.where(keep, 0.0, MASK_VALUE)

    lane = lax.broadcasted_iota(jnp.int32, (BLOCK_Q, LANES), 1)
    first_head = lane < HEAD_DIM

    def head_select(a, b):
        return jnp.where(first_head, a, b)

    def regroup(src, chunk, dst32, dst16):
        quarter = chunk // DILATION_STEP
        for c in range(seq // chunk):
            for r in range(DILATION_STEP):
                rows = src[pl.ds(c * chunk + r, quarter, stride=DILATION_STEP), :]
                lo = c * chunk + r * quarter
                if dst32 is not None:
                    dst32[lo:lo + quarter, :] = rows
                dst16[lo:lo + quarter, :] = rows.astype(bf16)

    sources = [(q_ref, k_ref, v_ref), (qp1, kp1, vp1), (qp2, kp2, vp2)]
    for x_ref, x32, x32g, xp1, xp2 in ((q_ref, q32, q32g, qp1, qp2), (k_ref, k32, k32g, kp1, kp2),
                                       (v_ref, v32, v32g, vp1, vp2)):
        x32[...] = x_ref[...].astype(f32)
        regroup(x32, seq, x32g, xp1)
        regroup(x32g, seq // DILATION_STEP, None, xp2)

    def level1_rows(level, i):
        if level < 2:
            return pl.ds(i * BLOCK_Q, BLOCK_Q)
        c, r = divmod(i, DILATION_STEP)
        return pl.ds(c * DILATION_STEP * BLOCK_Q + r, BLOCK_Q, stride=DILATION_STEP)

    def key_window(level, i):
        blocks_per_sub = n_blocks // DILATIONS[level]
        key_lo = i * BLOCK_Q if i % blocks_per_sub == 0 else (i - 1) * BLOCK_Q
        return key_lo, (i + 1) * BLOCK_Q - key_lo

    def probabilities(level, i):
        q_src, k_src, _ = sources[level]
        key_lo, n_keys = key_window(level, i)
        qb = q_src[i * BLOCK_Q:(i + 1) * BLOCK_Q, :]
        kw = k_src[key_lo:key_lo + n_keys, :]
        bias = bias_ref[:, 2 * BLOCK_Q - n_keys:]
        zero = jnp.zeros_like(qb)
        q2 = jnp.concatenate([jnp.where(first_head, qb, zero), jnp.where(first_head, zero, qb)], axis=0)
        s = lax.dot_general(q2, kw, (((1,), (1,)), ((), ())), preferred_element_type=f32)
        s = s + jnp.concatenate([bias, bias], axis=0)
        m = jnp.max(s, axis=-1, keepdims=True)
        return jnp.exp2(s - m).astype(bf16), m

    def weighted_values(level, i, e, m):
        key_lo, n_keys = key_window(level, i)
        vw = sources[level][2][key_lo:key_lo + n_keys, :]
        v_ext = jnp.concatenate([vw, jnp.ones((n_keys, LANES), bf16)], axis=1)
        o2 = jnp.dot(e, v_ext, preferred_element_type=f32)
        dst = level1_rows(level, i)
        m_wide = jnp.broadcast_to(m, (2 * BLOCK_Q, LANES))
        ot[level, dst, :] = head_select(o2[:BLOCK_Q, :LANES], o2[BLOCK_Q:, :LANES])
        dt[level, dst, :] = head_select(o2[:BLOCK_Q, LANES:], o2[BLOCK_Q:, LANES:])
        mt[level, dst, :] = head_select(m_wide[:BLOCK_Q], m_wide[BLOCK_Q:])

    quarter_blocks = n_blocks // DILATION_STEP

    def merge(b):
        r, j = divmod(b, quarter_blocks)
        rows0 = pl.ds(r + DILATION_STEP * BLOCK_Q * j, BLOCK_Q, stride=DILATION_STEP)
        rows1 = pl.ds(b * BLOCK_Q, BLOCK_Q)
        rows = [rows0] + [rows1] * (n_pat - 1)
        maxes = [mt[p, rows[p], :] for p in range(n_pat)]
        m_all = functools.reduce(jnp.maximum, maxes)
        num = jnp.zeros((BLOCK_Q, LANES), f32)
        den = jnp.zeros((BLOCK_Q, LANES), f32)
        for p in range(n_pat):
            w = jnp.exp2(maxes[p] - m_all)
            num = num + w * ot[p, rows[p], :]
            den = den + w * dt[p, rows[p], :]
        out32[rows0, :] = num / den

    head = n_blocks // 2
    order = [(0, i) for i in range(head)]
    for i in range(n_blocks - head):
        order += [(0, head + i), (2, 2 * i), (2, 2 * i + 1)]
    order += [(1, b) for b in range(n_blocks)]
    pending = {}
    for n in range(len(order) + ATTN_LOOKAHEAD):
        if n < len(order):
            pending[order[n]] = probabilities(*order[n])
        if n >= ATTN_LOOKAHEAD:
            level, i = order[n - ATTN_LOOKAHEAD]
            weighted_values(level, i, *pending.pop((level, i)))
            if level == 1:
                merge(i)
    o_ref[...] = out32[...].astype(o_ref.dtype)


def _attention(qkv, batch, seq):
    n_tok = qkv.shape[0]
    n_pairs = ATTN_WIDTH // LANES
    f32, bf16 = jnp.float32, jnp.bfloat16
    n_pat = len(DILATIONS)
    assert all(d == DILATION_STEP ** n for n, d in enumerate(DILATIONS)) and n_pat == 3
    assert seq % (DILATIONS[-1] * BLOCK_Q) == 0
    return pl.pallas_call(
        functools.partial(_attn_kernel, seq=seq),
        grid=(batch, n_pairs),
        in_specs=[
            pl.BlockSpec((seq, LANES), lambda b, h: (b, h)),
            pl.BlockSpec((seq, LANES), lambda b, h: (b, n_pairs + h)),
            pl.BlockSpec((seq, LANES), lambda b, h: (b, 2 * n_pairs + h)),
        ],
        out_specs=pl.BlockSpec((seq, LANES), lambda b, h: (b, h)),
        out_shape=jax.ShapeDtypeStruct((n_tok, ATTN_WIDTH), bf16),
        scratch_shapes=(
            [pltpu.VMEM((seq, LANES), f32)] * 6
            + [pltpu.VMEM((seq, LANES), bf16)] * 6
            + [pltpu.VMEM((BLOCK_Q, 2 * BLOCK_Q), f32)]
            + [pltpu.VMEM((n_pat, seq, LANES), f32)] * 3
            + [pltpu.VMEM((seq, LANES), f32)]
        ),
        compiler_params=pltpu.CompilerParams(
            dimension_semantics=("arbitrary", "arbitrary"), vmem_limit_bytes=VMEM_LIMIT_BYTES),
        name="attention",
    )(qkv, qkv, qkv)


def _gelu(x):
    return 0.5 * x * (1.0 + lax.erf(x * np.float32(1.0 / np.sqrt(2.0))))


def _sgu_pair_weights(w_ref):
    row = lax.broadcasted_iota(jnp.int32, (SGU_CHUNK, SGU_CHUNK), 0)
    col = lax.broadcasted_iota(jnp.int32, (SGU_CHUNK, SGU_CHUNK), 1)
    weights = [jnp.where(col <= row, w_ref[g], 0.0).astype(jnp.bfloat16) for g in range(SGU_GROUPS)]
    return [jnp.concatenate(weights[2 * sl:2 * sl + 2], axis=1) for sl in range(SGU_WIDTH // LANES)]


def _sgu_rows(u, v, gain, bias, pair_weights, b_mat):
    f32 = jnp.float32
    u = _gelu(u.astype(f32))
    v = _gelu(v.astype(f32))
    mu = jnp.mean(v, axis=-1, keepdims=True)
    vc = v - mu
    v = vc * lax.rsqrt(jnp.mean(vc * vc, axis=-1, keepdims=True) + LN_EPS)
    v = (v * gain + bias).astype(jnp.bfloat16)

    lane = lax.broadcasted_iota(jnp.int32, (SGU_CHUNK, LANES), 1)
    first_group = lane < (SGU_WIDTH // SGU_GROUPS)
    chunks = []
    for n in range(u.shape[0] // SGU_CHUNK):
        rows = slice(n * SGU_CHUNK, (n + 1) * SGU_CHUNK)
        slabs = []
        for sl in range(SGU_WIDTH // LANES):
            cols = slice(sl * LANES, (sl + 1) * LANES)
            vs = v[rows, cols]
            zero = jnp.zeros_like(vs)
            v_diag = jnp.concatenate([jnp.where(first_group, vs, zero), jnp.where(first_group, zero, vs)], axis=0)
            mixed = jnp.dot(pair_weights[sl], v_diag, preferred_element_type=f32) + b_mat[:, cols]
            slabs.append(u[rows, cols] * mixed)
        chunks.append(jnp.concatenate(slabs, axis=1))
    return jnp.concatenate(chunks, axis=0)


def _out_ffn_kernel(x_ref, attn_ref, sgu_ref, ga_ref, gs_ref, wo_ref, gpm_ref, gpf_ref,
                    wg_ref, wu_ref, wd_ref, gpo_ref, o_ref):
    f32 = jnp.float32
    bf16 = jnp.bfloat16
    sub = x_ref.shape[0] // FFN_SUBTILES
    row_slices = [slice(t * sub, (t + 1) * sub) for t in range(FFN_SUBTILES)]

    def out_proj(rows):
        a = _rms(attn_ref[rows, :].astype(f32), ga_ref[...]).astype(bf16)
        s = _rms(sgu_ref[rows, :].astype(f32), gs_ref[...]).astype(bf16)
        return (jnp.dot(a, wo_ref[0:ATTN_WIDTH, :], preferred_element_type=f32)
                + jnp.dot(s, wo_ref[ATTN_WIDTH:, :], preferred_element_type=f32))

    def ffn(rows, y):
        x1 = x_ref[rows, :] + _rms(y, gpm_ref[...])
        h = _rms(x1, gpf_ref[...]).astype(bf16)
        gate = jnp.dot(h, wg_ref[...], preferred_element_type=f32)
        up = jnp.dot(h, wu_ref[...], preferred_element_type=f32)
        act = (gate * jax.nn.sigmoid(gate) * up).astype(bf16)
        return x1, jnp.dot(act, wd_ref[...], preferred_element_type=f32)

    y_next = out_proj(row_slices[0])
    for t, rows in enumerate(row_slices):
        y = y_next
        if t + 1 < FFN_SUBTILES:
            y_next = out_proj(row_slices[t + 1])
        x1, f = ffn(rows, y)
        o_ref[rows, :] = x1 + _rms(f, gpo_ref[...])


def _out_ffn(x2, attn, sgu, ga, gs, wo, gpm, gpf, wg, wu, wd, gpo):
    n_tok = x2.shape[0]
    tm = FFN_TOKEN_TILE

    def const(shape):
        return pl.BlockSpec(shape, lambda i: (0,) * len(shape), pipeline_mode=pl.Buffered(1))

    return pl.pallas_call(
        _out_ffn_kernel,
        grid=(n_tok // tm,),
        in_specs=[
            pl.BlockSpec((tm, D_MODEL), lambda i: (i, 0)),
            pl.BlockSpec((tm, ATTN_WIDTH), lambda i: (i, 0)),
            pl.BlockSpec((tm, SGU_WIDTH), lambda i: (i, 0)),
            const((1, ATTN_WIDTH)),
            const((1, SGU_WIDTH)),
            const((D_MODEL, D_MODEL)),
            const((1, D_MODEL)),
            const((1, D_MODEL)),
            const((D_MODEL, D_FF)),
            const((D_MODEL, D_FF)),
            const((D_FF, D_MODEL)),
            const((1, D_MODEL)),
        ],
        out_specs=pl.BlockSpec((tm, D_MODEL), lambda i: (i, 0)),
        out_shape=jax.ShapeDtypeStruct((n_tok, D_MODEL), jnp.float32),
        compiler_params=pltpu.CompilerParams(
            dimension_semantics=("arbitrary",), vmem_limit_bytes=VMEM_LIMIT_BYTES),
        name="out_ffn",
    )(x2, attn, sgu, ga, gs, wo, gpm, gpf, wg, wu, wd, gpo)


def kernel(x, positions, pre_mix_norm, w_in, sgu_ln_gain, sgu_ln_bias, sgu_w_spatial, sgu_b_spatial, attn_out_norm, sgu_out_norm, w_out, post_mix_norm, pre_ffn_norm, w_gate, w_up, w_down, post_ffn_norm):
    batch, seq, _ = x.shape
    n_tok = batch * seq
    x2 = x.reshape(n_tok, D_MODEL)
    pos_row = positions.reshape(1, n_tok)

    depth = w_in.shape[0]
    for l in range(depth):
        b_mat = jnp.repeat(sgu_b_spatial[l].T, SGU_WIDTH // SGU_GROUPS, axis=1)
        qkv, sgu, (wo, wg, wu, wd) = _in_proj(
            x2, pos_row, pre_mix_norm[l][None, :], w_in[l],
            sgu_ln_gain[l][None, :], sgu_ln_bias[l][None, :], sgu_w_spatial[l], b_mat,
            later_weights=(w_out[l], w_gate[l], w_up[l], w_down[l]))
        attn = _attention(qkv, batch, seq)
        x2 = _out_ffn(x2, attn, sgu, attn_out_norm[l][None, :], sgu_out_norm[l][None, :],
                      wo, post_mix_norm[l][None, :], pre_ffn_norm[l][None, :], wg, wu, wd,
                      post_ffn_norm[l][None, :])
    return x2.reshape(batch, seq, D_MODEL)
```

```python
import functools

import jax
import jax.numpy as jnp
import numpy as np
from jax import lax
from jax.experimental import pallas as pl
from jax.experimental.pallas import tpu as pltpu

D_MODEL = 1024
HEAD_DIM = 64
ATTN_WIDTH = 512
SGU_WIDTH = 512
SGU_GROUPS = 8
SGU_CHUNK = 128
DILATIONS = (1, 4, 16)
WINDOW_STEPS = 128
BLOCK_Q = 128
DILATION_STEP = 4
ATTN_LOOKAHEAD = 3
SCORE_SCALE = float(np.float32(0.125 * np.log2(np.e)))
ROPE_THETA = 500000.0
ROT_DIM = 16
D_FF = 2816
RMS_EPS = 1e-6
LN_EPS = 1e-5
MASK_VALUE = float(np.finfo(np.float32).min)

LANES = 128
BF16_SUBLANE_TILE = 16
VMEM_LIMIT_BYTES = 56 * 1024 * 1024

IN_PROJ_TOKEN_TILE = 1024
IN_PROJ_SUBTILES = 4
FFN_TOKEN_TILE = 1024
FFN_SUBTILES = 2


def _rms(x, gain):
    return x * lax.rsqrt(jnp.mean(x * x, axis=-1, keepdims=True) + RMS_EPS) * gain


def _rotary_tables(pos_row, invf_col, spread):
    f32, bf16 = jnp.float32, jnp.bfloat16
    ang = invf_col * pos_row.astype(f32)
    trig = jnp.concatenate([jnp.cos(ang), jnp.sin(ang)], axis=0)
    hi = trig.astype(bf16)
    rest = trig - hi.astype(f32)
    mid = rest.astype(bf16)
    lo = (rest - mid.astype(f32)).astype(bf16)
    terms = jnp.concatenate([hi, mid, lo], axis=0)
    return lax.dot_general(terms, spread, (((0,), (0,)), ((), ())), preferred_element_type=f32)


def _in_proj_kernel(x_ref, pos_ref, gain_ref, invf_ref, spread_ref, w32_ref,
                    ln_gain_ref, ln_bias_ref, ws_ref, bs_ref, *rest):
    n_cast = len(rest) // 2 - 1
    cast_src, (qkv_ref, sgu_ref), cast_dst, w_ref = (
        rest[:n_cast], rest[n_cast:n_cast + 2], rest[n_cast + 2:2 * n_cast + 2], rest[-1])

    @pl.when(pl.program_id(0) == 0)
    def _():
        w_ref[...] = w32_ref[...].astype(w_ref.dtype)

    for src, dst in zip(cast_src, cast_dst):
        dst[...] = src[...].astype(dst.dtype)

    tables = _rotary_tables(pos_ref[...], invf_ref[...], spread_ref[...])
    pair_weights = _sgu_pair_weights(ws_ref)
    lane = lax.broadcasted_iota(jnp.int32, (1, LANES), 1)
    ones_off_rotary = jnp.where((lane % HEAD_DIM) < ROT_DIM, 0.0, 1.0)
    half = ROT_DIM // 2
    q_slabs = ATTN_WIDTH // LANES
    qk_width = 2 * ATTN_WIDTH
    uv_lo = 3 * ATTN_WIDTH

    sub = x_ref.shape[0] // IN_PROJ_SUBTILES
    row_slices = [slice(t * sub, (t + 1) * sub) for t in range(IN_PROJ_SUBTILES)]

    def project(rows):
        h = _rms(x_ref[rows, :], gain_ref[...]).astype(jnp.bfloat16)
        return jnp.dot(h, w_ref[...], preferred_element_type=jnp.float32)

    proj_next = project(row_slices[0])
    for t, rows in enumerate(row_slices):
        proj = proj_next
        if t + 1 < IN_PROJ_SUBTILES:
            proj_next = project(row_slices[t + 1])
        cos = tables[rows, :LANES] + ones_off_rotary
        sin_from_below = tables[rows, LANES:2 * LANES]
        sin_from_above = tables[rows, 2 * LANES:]
        for j in range(2 * q_slabs):
            slab = proj[:, j * LANES:(j + 1) * LANES]
            rot = (slab * cos
                   + pltpu.roll(slab, half, axis=1) * sin_from_below
                   + pltpu.roll(slab, LANES - half, axis=1) * sin_from_above)
            if j < q_slabs:
                rot = rot * SCORE_SCALE
            qkv_ref[rows, j * LANES:(j + 1) * LANES] = rot.astype(qkv_ref.dtype)
        qkv_ref[rows, qk_width:] = proj[:, qk_width:uv_lo].astype(qkv_ref.dtype)
        sgu = _sgu_rows(proj[:, uv_lo:uv_lo + SGU_WIDTH], proj[:, uv_lo + SGU_WIDTH:],
                        ln_gain_ref[...], ln_bias_ref[...], pair_weights, bs_ref[...])
        sgu_ref[rows, :] = sgu.astype(sgu_ref.dtype)


def _rotary_constants():
    n_freq = ROT_DIM // 2
    inv_freq = ROPE_THETA ** (-jnp.arange(0, ROT_DIM, 2, dtype=jnp.float32) / ROT_DIM)
    lane = np.arange(LANES)
    in_rot = (lane % HEAD_DIM) < ROT_DIM
    upper = (lane % ROT_DIM) >= n_freq
    hit = (lane[None, :] % n_freq) == np.arange(n_freq)[:, None]
    cos_rows = np.concatenate([hit & in_rot, np.zeros_like(hit), np.zeros_like(hit)], axis=1)
    sin_rows = np.concatenate([np.zeros(hit.shape), 1.0 * (hit & in_rot & upper),
                               -1.0 * (hit & in_rot & ~upper)], axis=1)
    one_term = np.concatenate([cos_rows.astype(np.float32), sin_rows.astype(np.float32)], axis=0)
    spread = np.concatenate([one_term] * 3, axis=0)
    return inv_freq.reshape(n_freq, 1), jnp.asarray(spread, dtype=jnp.bfloat16)


def _in_proj(x2, pos_row, gain, w_in, ln_gain, ln_bias, w_spatial, b_mat, later_weights):
    n_tok = x2.shape[0]
    tm = IN_PROJ_TOKEN_TILE
    n_steps = n_tok // tm
    width = w_in.shape[1]
    bf16 = jnp.bfloat16
    invf, spread = _rotary_constants()
    assert (tm // IN_PROJ_SUBTILES) % SGU_CHUNK == 0

    def const(shape, **kwargs):
        return pl.BlockSpec(shape, lambda i: (0,) * len(shape), **kwargs)

    def row_slab(w):
        rows = w.shape[0] // n_steps
        assert rows * n_steps == w.shape[0] and rows % BF16_SUBLANE_TILE == 0
        return pl.BlockSpec((rows, w.shape[1]), lambda i: (i, 0))

    slabs = [row_slab(w) for w in later_weights]
    outs = pl.pallas_call(
        _in_proj_kernel,
        grid=(n_steps,),
        in_specs=[
            pl.BlockSpec((tm, D_MODEL), lambda i: (i, 0)),
            pl.BlockSpec((1, tm), lambda i: (0, i)),
            const((1, D_MODEL)),
            const(invf.shape),
            const(spread.shape),
            const((D_MODEL, width), pipeline_mode=pl.Buffered(1)),
            const((1, SGU_WIDTH)),
            const((1, SGU_WIDTH)),
            const((SGU_GROUPS, SGU_CHUNK, SGU_CHUNK)),
            const((SGU_CHUNK, SGU_WIDTH)),
        ] + slabs,
        out_specs=[
            pl.BlockSpec((tm, 3 * ATTN_WIDTH), lambda i: (i, 0)),
            pl.BlockSpec((tm, SGU_WIDTH), lambda i: (i, 0)),
        ] + slabs,
        out_shape=[
            jax.ShapeDtypeStruct((n_tok, 3 * ATTN_WIDTH), bf16),
            jax.ShapeDtypeStruct((n_tok, SGU_WIDTH), bf16),
        ] + [jax.ShapeDtypeStruct(w.shape, bf16) for w in later_weights],
        scratch_shapes=[pltpu.VMEM((D_MODEL, width), bf16)],
        compiler_params=pltpu.CompilerParams(
            dimension_semantics=("arbitrary",), vmem_limit_bytes=VMEM_LIMIT_BYTES),
        name="in_proj",
    )(x2, pos_row, gain, invf, spread, w_in, ln_gain, ln_bias, w_spatial, b_mat, *later_weights)
    return outs[0], outs[1], outs[2:]


def _attn_kernel(q_ref, k_ref, v_ref, o_ref,
                 q32, k32, v32, q32g, k32g, v32g, qp1, kp1, vp1, qp2, kp2, vp2,
                 bias_ref, ot, mt, dt, out32, *, seq):
    n_blocks = seq // BLOCK_Q
    n_pat = len(DILATIONS)
    f32 = jnp.float32
    bf16 = jnp.bfloat16

    row = lax.broadcasted_iota(jnp.int32, (BLOCK_Q, 2 * BLOCK_Q), 0)
    col = lax.broadcasted_iota(jnp.int32, (BLOCK_Q, 2 * BLOCK_Q), 1)
    keep = (col >= row) & (col <= row + WINDOW_STEPS)
    bias_ref[...] = jnp.where(keep, 0.0, MASK_VALUE)

    lane = lax.broadcasted_iota(jnp.int32, (BLOCK_Q, LANES), 1)
    first_head = lane < HEAD_DIM

    def head_select(a, b):
        return jnp.where(first_head, a, b)

    def regroup(src, chunk, dst32, dst16):
        quarter = chunk // DILATION_STEP
        for c in range(seq // chunk):
            for r in range(DILATION_STEP):
                rows = src[pl.ds(c * chunk + r, quarter, stride=DILATION_STEP), :]
                lo = c * chunk + r * quarter
                if dst32 is not None:
                    dst32[lo:lo + quarter, :] = rows
                dst16[lo:lo + quarter, :] = rows.astype(bf16)

    sources = [(q_ref, k_ref, v_ref), (qp1, kp1, vp1), (qp2, kp2, vp2)]
    for x_ref, x32, x32g, xp1, xp2 in ((q_ref, q32, q32g, qp1, qp2), (k_ref, k32, k32g, kp1, kp2),
                                       (v_ref, v32, v32g, vp1, vp2)):
        x32[...] = x_ref[...].astype(f32)
        regroup(x32, seq, x32g, xp1)
        regroup(x32g, seq // DILATION_STEP, None, xp2)

    def level1_rows(level, i):
        if level < 2:
            return pl.ds(i * BLOCK_Q, BLOCK_Q)
        c, r = divmod(i, DILATION_STEP)
        return pl.ds(c * DILATION_STEP * BLOCK_Q + r, BLOCK_Q, stride=DILATION_STEP)

    def key_window(level, i):
        blocks_per_sub = n_blocks // DILATIONS[level]
        key_lo = i * BLOCK_Q if i % blocks_per_sub == 0 else (i - 1) * BLOCK_Q
        return key_lo, (i + 1) * BLOCK_Q - key_lo

    def probabilities(level, i):
        q_src, k_src, _ = sources[level]
        key_lo, n_keys = key_window(level, i)
        qb = q_src[i * BLOCK_Q:(i + 1) * BLOCK_Q, :]
        kw = k_src[key_lo:key_lo + n_keys, :]
        bias = bias_ref[:, 2 * BLOCK_Q - n_keys:]
        zero = jnp.zeros_like(qb)
        q2 = jnp.concatenate([jnp.where(first_head, qb, zero), jnp.where(first_head, zero, qb)], axis=0)
        s = lax.dot_general(q2, kw, (((1,), (1,)), ((), ())), preferred_element_type=f32)
        s = s + jnp.concatenate([bias, bias], axis=0)
        m = jnp.max(s, axis=-1, keepdims=True)
        return jnp.exp2(s - m).astype(bf16), m

    def weighted_values(level, i, e, m):
        key_lo, n_keys = key_window(level, i)
        vw = sources[level][2][key_lo:key_lo + n_keys, :]
        v_ext = jnp.concatenate([vw, jnp.ones((n_keys, LANES), bf16)], axis=1)
        o2 = jnp.dot(e, v_ext, preferred_element_type=f32)
        dst = level1_rows(level, i)
        m_wide = jnp.broadcast_to(m, (2 * BLOCK_Q, LANES))
        ot[level, dst, :] = head_select(o2[:BLOCK_Q, :LANES], o2[BLOCK_Q:, :LANES])
        dt[level, dst, :] = head_select(o2[:BLOCK_Q, LANES:], o2[BLOCK_Q:, LANES:])
        mt[level, dst, :] = head_select(m_wide[:BLOCK_Q], m_wide[BLOCK_Q:])

    quarter_blocks = n_blocks // DILATION_STEP

    def merge(b):
        r, j = divmod(b, quarter_blocks)
        rows0 = pl.ds(r + DILATION_STEP * BLOCK_Q * j, BLOCK_Q, stride=DILATION_STEP)
        rows1 = pl.ds(b * BLOCK_Q, BLOCK_Q)
        rows = [rows0] + [rows1] * (n_pat - 1)
        maxes = [mt[p, rows[p], :] for p in range(n_pat)]
        m_all = functools.reduce(jnp.maximum, maxes)
        num = jnp.zeros((BLOCK_Q, LANES), f32)
        den = jnp.zeros((BLOCK_Q, LANES), f32)
        for p in range(n_pat):
            w = jnp.exp2(maxes[p] - m_all)
            num = num + w * ot[p, rows[p], :]
            den = den + w * dt[p, rows[p], :]
        out32[rows0, :] = num / den

    head = n_blocks // 2
    order = [(0, i) for i in range(head)]
    for i in range(n_blocks - head):
        order += [(0, head + i), (2, 2 * i), (2, 2 * i + 1)]
    order += [(1, b) for b in range(n_blocks)]
    pending = {}
    for n in range(len(order) + ATTN_LOOKAHEAD):
        if n < len(order):
            pending[order[n]] = probabilities(*order[n])
        if n >= ATTN_LOOKAHEAD:
            level, i = order[n - ATTN_LOOKAHEAD]
            weighted_values(level, i, *pending.pop((level, i)))
            if level == 1:
                merge(i)
    o_ref[...] = out32[...].astype(o_ref.dtype)


def _attention(qkv, batch, seq):
    n_tok = qkv.shape[0]
    n_pairs = ATTN_WIDTH // LANES
    f32, bf16 = jnp.float32, jnp.bfloat16
    n_pat = len(DILATIONS)
    assert all(d == DILATION_STEP ** n for n, d in enumerate(DILATIONS)) and n_pat == 3
    assert seq % (DILATIONS[-1] * BLOCK_Q) == 0
    return pl.pallas_call(
        functools.partial(_attn_kernel, seq=seq),
        grid=(batch, n_pairs),
        in_specs=[
            pl.BlockSpec((seq, LANES), lambda b, h: (b, h)),
            pl.BlockSpec((seq, LANES), lambda b, h: (b, n_pairs + h)),
            pl.BlockSpec((seq, LANES), lambda b, h: (b, 2 * n_pairs + h)),
        ],
        out_specs=pl.BlockSpec((seq, LANES), lambda b, h: (b, h)),
        out_shape=jax.ShapeDtypeStruct((n_tok, ATTN_WIDTH), bf16),
        scratch_shapes=(
            [pltpu.VMEM((seq, LANES), f32)] * 6
            + [pltpu.VMEM((seq, LANES), bf16)] * 6
            + [pltpu.VMEM((BLOCK_Q, 2 * BLOCK_Q), f32)]
            + [pltpu.VMEM((n_pat, seq, LANES), f32)] * 3
            + [pltpu.VMEM((seq, LANES), f32)]
        ),
        compiler_params=pltpu.CompilerParams(
            dimension_semantics=("arbitrary", "arbitrary"), vmem_limit_bytes=VMEM_LIMIT_BYTES),
        name="attention",
    )(qkv, qkv, qkv)


def _gelu(x):
    return 0.5 * x * (1.0 + lax.erf(x * np.float32(1.0 / np.sqrt(2.0))))


def _sgu_pair_weights(w_ref):
    row = lax.broadcasted_iota(jnp.int32, (SGU_CHUNK, SGU_CHUNK), 0)
    col = lax.broadcasted_iota(jnp.int32, (SGU_CHUNK, SGU_CHUNK), 1)
    weights = [jnp.where(col <= row, w_ref[g], 0.0).astype(jnp.bfloat16) for g in range(SGU_GROUPS)]
    return [jnp.concatenate(weights[2 * sl:2 * sl + 2], axis=1) for sl in range(SGU_WIDTH // LANES)]


def _sgu_rows(u, v, gain, bias, pair_weights, b_mat):
    f32 = jnp.float32
    u = _gelu(u.astype(f32))
    v = _gelu(v.astype(f32))
    mu = jnp.mean(v, axis=-1, keepdims=True)
    vc = v - mu
    v = vc * lax.rsqrt(jnp.mean(vc * vc, axis=-1, keepdims=True) + LN_EPS)
    v = (v * gain + bias).astype(jnp.bfloat16)

    lane = lax.broadcasted_iota(jnp.int32, (SGU_CHUNK, LANES), 1)
    first_group = lane < (SGU_WIDTH // SGU_GROUPS)
    chunks = []
    for n in range(u.shape[0] // SGU_CHUNK):
        rows = slice(n * SGU_CHUNK, (n + 1) * SGU_CHUNK)
        slabs = []
        for sl in range(SGU_WIDTH // LANES):
            cols = slice(sl * LANES, (sl + 1) * LANES)
            vs = v[rows, cols]
            zero = jnp.zeros_like(vs)
            v_diag = jnp.concatenate([jnp.where(first_group, vs, zero), jnp.where(first_group, zero, vs)], axis=0)
            mixed = jnp.dot(pair_weights[sl], v_diag, preferred_element_type=f32) + b_mat[:, cols]
            slabs.append(u[rows, cols] * mixed)
        chunks.append(jnp.concatenate(slabs, axis=1))
    return jnp.concatenate(chunks, axis=0)


def _out_ffn_kernel(x_ref, attn_ref, sgu_ref, ga_ref, gs_ref, wo_ref, gpm_ref, gpf_ref,
                    wg_ref, wu_ref, wd_ref, gpo_ref, o_ref):
    f32 = jnp.float32
    bf16 = jnp.bfloat16
    sub = x_ref.shape[0] // FFN_SUBTILES
    row_slices = [slice(t * sub, (t + 1) * sub) for t in range(FFN_SUBTILES)]

    def out_proj(rows):
        a = _rms(attn_ref[rows, :].astype(f32), ga_ref[...]).astype(bf16)
        s = _rms(sgu_ref[rows, :].astype(f32), gs_ref[...]).astype(bf16)
        return (jnp.dot(a, wo_ref[0:ATTN_WIDTH, :], preferred_element_type=f32)
                + jnp.dot(s, wo_ref[ATTN_WIDTH:, :], preferred_element_type=f32))

    def ffn(rows, y):
        x1 = x_ref[rows, :] + _rms(y, gpm_ref[...])
        h = _rms(x1, gpf_ref[...]).astype(bf16)
        gate = jnp.dot(h, wg_ref[...], preferred_element_type=f32)
        up = jnp.dot(h, wu_ref[...], preferred_element_type=f32)
        act = (gate * jax.nn.sigmoid(gate) * up).astype(bf16)
        return x1, jnp.dot(act, wd_ref[...], preferred_element_type=f32)

    y_next = out_proj(row_slices[0])
    for t, rows in enumerate(row_slices):
        y = y_next
        if t + 1 < FFN_SUBTILES:
            y_next = out_proj(row_slices[t + 1])
        x1, f = ffn(rows, y)
        o_ref[rows, :] = x1 + _rms(f, gpo_ref[...])


def _out_ffn(x2, attn, sgu, ga, gs, wo, gpm, gpf, wg, wu, wd, gpo):
    n_tok = x2.shape[0]
    tm = FFN_TOKEN_TILE

    def const(shape):
        return pl.BlockSpec(shape, lambda i: (0,) * len(shape), pipeline_mode=pl.Buffered(1))

    return pl.pallas_call(
        _out_ffn_kernel,
        grid=(n_tok // tm,),
        in_specs=[
            pl.BlockSpec((tm, D_MODEL), lambda i: (i, 0)),
            pl.BlockSpec((tm, ATTN_WIDTH), lambda i: (i, 0)),
            pl.BlockSpec((tm, SGU_WIDTH), lambda i: (i, 0)),
            const((1, ATTN_WIDTH)),
            const((1, SGU_WIDTH)),
            const((D_MODEL, D_MODEL)),
            const((1, D_MODEL)),
            const((1, D_MODEL)),
            const((D_MODEL, D_FF)),
            const((D_MODEL, D_FF)),
            const((D_FF, D_MODEL)),
            const((1, D_MODEL)),
        ],
        out_specs=pl.BlockSpec((tm, D_MODEL), lambda i: (i, 0)),
        out_shape=jax.ShapeDtypeStruct((n_tok, D_MODEL), jnp.float32),
        compiler_params=pltpu.CompilerParams(
            dimension_semantics=("arbitrary",), vmem_limit_bytes=VMEM_LIMIT_BYTES),
        name="out_ffn",
    )(x2, attn, sgu, ga, gs, wo, gpm, gpf, wg, wu, wd, gpo)


def kernel(x, positions, pre_mix_norm, w_in, sgu_ln_gain, sgu_ln_bias, sgu_w_spatial, sgu_b_spatial, attn_out_norm, sgu_out_norm, w_out, post_mix_norm, pre_ffn_norm, w_gate, w_up, w_down, post_ffn_norm):
    batch, seq, _ = x.shape
    n_tok = batch * seq
    x2 = x.reshape(n_tok, D_MODEL)
    pos_row = positions.reshape(1, n_tok)

    depth = w_in.shape[0]
    for l in range(depth):
        b_mat = jnp.repeat(sgu_b_spatial[l].T, SGU_WIDTH // SGU_GROUPS, axis=1)
        qkv, sgu, (wo, wg, wu, wd) = _in_proj(
            x2, pos_row, pre_mix_norm[l][None, :], w_in[l],
            sgu_ln_gain[l][None, :], sgu_ln_bias[l][None, :], sgu_w_spatial[l], b_mat,
            later_weights=(w_out[l], w_gate[l], w_up[l], w_down[l]))
        attn = _attention(qkv, batch, seq)
        x2 = _out_ffn(x2, attn, sgu, attn_out_norm[l][None, :], sgu_out_norm[l][None, :],
                      wo, post_mix_norm[l][None, :], pre_ffn_norm[l][None, :], wg, wu, wd,
                      post_ffn_norm[l][None, :])
    return x2.reshape(batch, seq, D_MODEL)
```

```python
import functools

import jax
import jax.numpy as jnp
import numpy as np
from jax import lax
from jax.experimental import pallas as pl
from jax.experimental.pallas import tpu as pltpu

D_MODEL = 1024
HEAD_DIM = 64
ATTN_WIDTH = 512
SGU_WIDTH = 512
SGU_GROUPS = 8
SGU_CHUNK = 128
DILATIONS = (1, 4, 16)
WINDOW_STEPS = 128
BLOCK_Q = 128
DILATION_STEP = 4
ATTN_LOOKAHEAD = 3
SCORE_SCALE = float(np.float32(0.125 * np.log2(np.e)))
ROPE_THETA = 500000.0
ROT_DIM = 16
D_FF = 2816
RMS_EPS = 1e-6
LN_EPS = 1e-5
MASK_VALUE = float(np.finfo(np.float32).min)

LANES = 128
BF16_SUBLANE_TILE = 16
VMEM_LIMIT_BYTES = 56 * 1024 * 1024

IN_PROJ_TOKEN_TILE = 1024
IN_PROJ_SUBTILES = 4
FFN_TOKEN_TILE = 1024
FFN_SUBTILES = 4


def _rms(x, gain):
    return x * lax.rsqrt(jnp.mean(x * x, axis=-1, keepdims=True) + RMS_EPS) * gain


def _rotary_tables(pos_row, invf_col, spread):
    f32, bf16 = jnp.float32, jnp.bfloat16
    ang = invf_col * pos_row.astype(f32)
    trig = jnp.concatenate([jnp.cos(ang), jnp.sin(ang)], axis=0)
    hi = trig.astype(bf16)
    rest = trig - hi.astype(f32)
    mid = rest.astype(bf16)
    lo = (rest - mid.astype(f32)).astype(bf16)
    terms = jnp.concatenate([hi, mid, lo], axis=0)
    return lax.dot_general(terms, spread, (((0,), (0,)), ((), ())), preferred_element_type=f32)


def _in_proj_kernel(x_ref, pos_ref, gain_ref, invf_ref, spread_ref, w32_ref,
                    ln_gain_ref, ln_bias_ref, ws_ref, bs_ref, *rest):
    n_cast = len(rest) // 2 - 1
    cast_src, (qkv_ref, sgu_ref), cast_dst, w_ref = (
        rest[:n_cast], rest[n_cast:n_cast + 2], rest[n_cast + 2:2 * n_cast + 2], rest[-1])

    @pl.when(pl.program_id(0) == 0)
    def _():
        w_ref[...] = w32_ref[...].astype(w_ref.dtype)

    for src, dst in zip(cast_src, cast_dst):
        dst[...] = src[...].astype(dst.dtype)

    tables = _rotary_tables(pos_ref[...], invf_ref[...], spread_ref[...])
    pair_weights = _sgu_pair_weights(ws_ref)
    lane = lax.broadcasted_iota(jnp.int32, (1, LANES), 1)
    ones_off_rotary = jnp.where((lane % HEAD_DIM) < ROT_DIM, 0.0, 1.0)
    half = ROT_DIM // 2
    q_slabs = ATTN_WIDTH // LANES
    qk_width = 2 * ATTN_WIDTH
    uv_lo = 3 * ATTN_WIDTH

    sub = x_ref.shape[0] // IN_PROJ_SUBTILES
    row_slices = [slice(t * sub, (t + 1) * sub) for t in range(IN_PROJ_SUBTILES)]

    def project(rows):
        h = _rms(x_ref[rows, :], gain_ref[...]).astype(jnp.bfloat16)
        return jnp.dot(h, w_ref[...], preferred_element_type=jnp.float32)

    proj_next = project(row_slices[0])
    for t, rows in enumerate(row_slices):
        proj = proj_next
        if t + 1 < IN_PROJ_SUBTILES:
            proj_next = project(row_slices[t + 1])
        cos = tables[rows, :LANES] + ones_off_rotary
        sin_from_below = tables[rows, LANES:2 * LANES]
        sin_from_above = tables[rows, 2 * LANES:]
        for j in range(2 * q_slabs):
            slab = proj[:, j * LANES:(j + 1) * LANES]
            rot = (slab * cos
                   + pltpu.roll(slab, half, axis=1) * sin_from_below
                   + pltpu.roll(slab, LANES - half, axis=1) * sin_from_above)
            if j < q_slabs:
                rot = rot * SCORE_SCALE
            qkv_ref[rows, j * LANES:(j + 1) * LANES] = rot.astype(qkv_ref.dtype)
        qkv_ref[rows, qk_width:] = proj[:, qk_width:uv_lo].astype(qkv_ref.dtype)
        sgu = _sgu_rows(proj[:, uv_lo:uv_lo + SGU_WIDTH], proj[:, uv_lo + SGU_WIDTH:],
                        ln_gain_ref[...], ln_bias_ref[...], pair_weights, bs_ref[...])
        sgu_ref[rows, :] = sgu.astype(sgu_ref.dtype)


def _rotary_constants():
    n_freq = ROT_DIM // 2
    inv_freq = ROPE_THETA ** (-jnp.arange(0, ROT_DIM, 2, dtype=jnp.float32) / ROT_DIM)
    lane = np.arange(LANES)
    in_rot = (lane % HEAD_DIM) < ROT_DIM
    upper = (lane % ROT_DIM) >= n_freq
    hit = (lane[None, :] % n_freq) == np.arange(n_freq)[:, None]
    cos_rows = np.concatenate([hit & in_rot, np.zeros_like(hit), np.zeros_like(hit)], axis=1)
    sin_rows = np.concatenate([np.zeros(hit.shape), 1.0 * (hit & in_rot & upper),
                               -1.0 * (hit & in_rot & ~upper)], axis=1)
    one_term = np.concatenate([cos_rows.astype(np.float32), sin_rows.astype(np.float32)], axis=0)
    spread = np.concatenate([one_term] * 3, axis=0)
    return inv_freq.reshape(n_freq, 1), jnp.asarray(spread, dtype=jnp.bfloat16)


def _in_proj(x2, pos_row, gain, w_in, ln_gain, ln_bias, w_spatial, b_mat, later_weights):
    n_tok = x2.shape[0]
    tm = IN_PROJ_TOKEN_TILE
    n_steps = n_tok // tm
    width = w_in.shape[1]
    bf16 = jnp.bfloat16
    invf, spread = _rotary_constants()
    assert (tm // IN_PROJ_SUBTILES) % SGU_CHUNK == 0

    def const(shape, **kwargs):
        return pl.BlockSpec(shape, lambda i: (0,) * len(shape), **kwargs)

    def row_slab(w):
        rows = w.shape[0] // n_steps
        assert rows * n_steps == w.shape[0] and rows % BF16_SUBLANE_TILE == 0
        return pl.BlockSpec((rows, w.shape[1]), lambda i: (i, 0))

    slabs = [row_slab(w) for w in later_weights]
    outs = pl.pallas_call(
        _in_proj_kernel,
        grid=(n_steps,),
        in_specs=[
            pl.BlockSpec((tm, D_MODEL), lambda i: (i, 0)),
            pl.BlockSpec((1, tm), lambda i: (0, i)),
            const((1, D_MODEL)),
            const(invf.shape),
            const(spread.shape),
            const((D_MODEL, width), pipeline_mode=pl.Buffered(1)),
            const((1, SGU_WIDTH)),
            const((1, SGU_WIDTH)),
            const((SGU_GROUPS, SGU_CHUNK, SGU_CHUNK)),
            const((SGU_CHUNK, SGU_WIDTH)),
        ] + slabs,
        out_specs=[
            pl.BlockSpec((tm, 3 * ATTN_WIDTH), lambda i: (i, 0)),
            pl.BlockSpec((tm, SGU_WIDTH), lambda i: (i, 0)),
        ] + slabs,
        out_shape=[
            jax.ShapeDtypeStruct((n_tok, 3 * ATTN_WIDTH), bf16),
            jax.ShapeDtypeStruct((n_tok, SGU_WIDTH), bf16),
        ] + [jax.ShapeDtypeStruct(w.shape, bf16) for w in later_weights],
        scratch_shapes=[pltpu.VMEM((D_MODEL, width), bf16)],
        compiler_params=pltpu.CompilerParams(
            dimension_semantics=("arbitrary",), vmem_limit_bytes=VMEM_LIMIT_BYTES),
        name="in_proj",
    )(x2, pos_row, gain, invf, spread, w_in, ln_gain, ln_bias, w_spatial, b_mat, *later_weights)
    return outs[0], outs[1], outs[2:]


def _attn_kernel(q_ref, k_ref, v_ref, o_ref,
                 q32, k32, v32, q32g, k32g, v32g, qp1, kp1, vp1, qp2, kp2, vp2,
                 bias_ref, ot, mt, dt, out32, *, seq):
    n_blocks = seq // BLOCK_Q
    n_pat = len(DILATIONS)
    f32 = jnp.float32
    bf16 = jnp.bfloat16

    row = lax.broadcasted_iota(jnp.int32, (BLOCK_Q, 2 * BLOCK_Q), 0)
    col = lax.broadcasted_iota(jnp.int32, (BLOCK_Q, 2 * BLOCK_Q), 1)
    keep = (col >= row) & (col <= row + WINDOW_STEPS)
    bias_ref[...] = jnp.where(keep, 0.0, MASK_VALUE)

    lane = lax.broadcasted_iota(jnp.int32, (BLOCK_Q, LANES), 1)
    first_head = lane < HEAD_DIM

    def head_select(a, b):
        return jnp.where(first_head, a, b)

    head_ones = [jnp.where(first_head, 1.0, 0.0).astype(bf16), jnp.where(first_head, 0.0, 1.0).astype(bf16)]

    def regroup(src, chunk, dst32, dst16):
        quarter = chunk // DILATION_STEP
        for c in range(seq // chunk):
            for r in range(DILATION_STEP):
                rows = src[pl.ds(c * chunk + r, quarter, stride=DILATION_STEP), :]
                lo = c * chunk + r * quarter
                if dst32 is not None:
                    dst32[lo:lo + quarter, :] = rows
                dst16[lo:lo + quarter, :] = rows.astype(bf16)

    sources = [(q_ref, k_ref, v_ref), (qp1, kp1, vp1), (qp2, kp2, vp2)]
    for x_ref, x32, x32g, xp1, xp2 in ((q_ref, q32, q32g, qp1, qp2), (k_ref, k32, k32g, kp1, kp2),
                                       (v_ref, v32, v32g, vp1, vp2)):
        x32[...] = x_ref[...].astype(f32)
        regroup(x32, seq, x32g, xp1)
        regroup(x32g, seq // DILATION_STEP, None, xp2)

    def level1_rows(level, i):
        if level < 2:
            return pl.ds(i * BLOCK_Q, BLOCK_Q)
        c, r = divmod(i, DILATION_STEP)
        return pl.ds(c * DILATION_STEP * BLOCK_Q + r, BLOCK_Q, stride=DILATION_STEP)

    def key_window(level, i):
        blocks_per_sub = n_blocks // DILATIONS[level]
        key_lo = i * BLOCK_Q if i % blocks_per_sub == 0 else (i - 1) * BLOCK_Q
        return key_lo, (i + 1) * BLOCK_Q - key_lo

    def probabilities(level, i):
        q_src, k_src, _ = sources[level]
        key_lo, n_keys = key_window(level, i)
        qb = q_src[i * BLOCK_Q:(i + 1) * BLOCK_Q, :]
        kw = k_src[key_lo:key_lo + n_keys, :]
        bias = bias_ref[:, 2 * BLOCK_Q - n_keys:]
        zero = jnp.zeros_like(qb)
        if n_keys == BLOCK_Q:
            k2 = jnp.concatenate([jnp.where(first_head, kw, zero), jnp.where(first_head, zero, kw)], axis=0)
            s = lax.dot_general(qb, k2, (((1,), (1,)), ((), ())), preferred_element_type=f32)
            halves = [s[:, h * BLOCK_Q:(h + 1) * BLOCK_Q] + bias for h in range(2)]
            maxes = [jnp.max(x, axis=-1, keepdims=True) for x in halves]
            e = jnp.concatenate([jnp.exp2(x - m) for x, m in zip(halves, maxes)], axis=1)
            m_tile = head_select(*[jnp.broadcast_to(m, (BLOCK_Q, LANES)) for m in maxes])
            return e.astype(bf16), m_tile
        q2 = jnp.concatenate([jnp.where(first_head, qb, zero), jnp.where(first_head, zero, qb)], axis=0)
        s = lax.dot_general(q2, kw, (((1,), (1,)), ((), ())), preferred_element_type=f32)
        s = s + jnp.concatenate([bias, bias], axis=0)
        m = jnp.max(s, axis=-1, keepdims=True)
        m_wide = jnp.broadcast_to(m, (2 * BLOCK_Q, LANES))
        m_tile = head_select(m_wide[:BLOCK_Q], m_wide[BLOCK_Q:])
        return jnp.exp2(s - m).astype(bf16), m_tile

    def weighted_values(level, i, e, m_tile):
        key_lo, n_keys = key_window(level, i)
        vw = sources[level][2][key_lo:key_lo + n_keys, :]
        dst = level1_rows(level, i)
        mt[level, dst, :] = m_tile
        if n_keys == BLOCK_Q:
            zero = jnp.zeros_like(vw)
            v2 = jnp.concatenate([
                jnp.concatenate([jnp.where(first_head, vw, zero), head_ones[0]], axis=1),
                jnp.concatenate([jnp.where(first_head, zero, vw), head_ones[1]], axis=1)], axis=0)
            o2 = jnp.dot(e, v2, preferred_element_type=f32)
            ot[level, dst, :] = o2[:, :LANES]
            dt[level, dst, :] = o2[:, LANES:]
            return
        v_ext = jnp.concatenate([vw, jnp.ones((n_keys, LANES), bf16)], axis=1)
        o2 = jnp.dot(e, v_ext, preferred_element_type=f32)
        ot[level, dst, :] = head_select(o2[:BLOCK_Q, :LANES], o2[BLOCK_Q:, :LANES])
        dt[level, dst, :] = head_select(o2[:BLOCK_Q, LANES:], o2[BLOCK_Q:, LANES:])

    quarter_blocks = n_blocks // DILATION_STEP

    def merge(b):
        r, j = divmod(b, quarter_blocks)
        rows0 = pl.ds(r + DILATION_STEP * BLOCK_Q * j, BLOCK_Q, stride=DILATION_STEP)
        rows1 = pl.ds(b * BLOCK_Q, BLOCK_Q)
        rows = [rows0] + [rows1] * (n_pat - 1)
        maxes = [mt[p, rows[p], :] for p in range(n_pat)]
        m_all = functools.reduce(jnp.maximum, maxes)
        num = jnp.zeros((BLOCK_Q, LANES), f32)
        den = jnp.zeros((BLOCK_Q, LANES), f32)
        for p in range(n_pat):
            w = jnp.exp2(maxes[p] - m_all)
            num = num + w * ot[p, rows[p], :]
            den = den + w * dt[p, rows[p], :]
        out32[rows0, :] = num / den

    head = n_blocks // 2
    order = [(0, i) for i in range(head)]
    for i in range(n_blocks - head):
        order += [(0, head + i), (2, 2 * i), (2, 2 * i + 1)]
    order += [(1, b) for b in range(n_blocks)]
    pending = {}
    for n in range(len(order) + ATTN_LOOKAHEAD):
        if n < len(order):
            pending[order[n]] = probabilities(*order[n])
        if n >= ATTN_LOOKAHEAD:
            level, i = order[n - ATTN_LOOKAHEAD]
            weighted_values(level, i, *pending.pop((level, i)))
            if level == 1:
                merge(i)
    o_ref[...] = out32[...].astype(o_ref.dtype)


def _attention(qkv, batch, seq):
    n_tok = qkv.shape[0]
    n_pairs = ATTN_WIDTH // LANES
    f32, bf16 = jnp.float32, jnp.bfloat16
    n_pat = len(DILATIONS)
    assert all(d == DILATION_STEP ** n for n, d in enumerate(DILATIONS)) and n_pat == 3
    assert seq % (DILATIONS[-1] * BLOCK_Q) == 0
    return pl.pallas_call(
        functools.partial(_attn_kernel, seq=seq),
        grid=(batch, n_pairs),
        in_specs=[
            pl.BlockSpec((seq, LANES), lambda b, h: (b, h)),
            pl.BlockSpec((seq, LANES), lambda b, h: (b, n_pairs + h)),
            pl.BlockSpec((seq, LANES), lambda b, h: (b, 2 * n_pairs + h)),
        ],
        out_specs=pl.BlockSpec((seq, LANES), lambda b, h: (b, h)),
        out_shape=jax.ShapeDtypeStruct((n_tok, ATTN_WIDTH), bf16),
        scratch_shapes=(
            [pltpu.VMEM((seq, LANES), f32)] * 6
            + [pltpu.VMEM((seq, LANES), bf16)] * 6
            + [pltpu.VMEM((BLOCK_Q, 2 * BLOCK_Q), f32)]
            + [pltpu.VMEM((n_pat, seq, LANES), f32)] * 3
            + [pltpu.VMEM((seq, LANES), f32)]
        ),
        compiler_params=pltpu.CompilerParams(
            dimension_semantics=("arbitrary", "arbitrary"), vmem_limit_bytes=VMEM_LIMIT_BYTES),
        name="attention",
    )(qkv, qkv, qkv)


def _gelu(x):
    return 0.5 * x * (1.0 + lax.erf(x * np.float32(1.0 / np.sqrt(2.0))))


def _sgu_pair_weights(w_ref):
    row = lax.broadcasted_iota(jnp.int32, (SGU_CHUNK, SGU_CHUNK), 0)
    col = lax.broadcasted_iota(jnp.int32, (SGU_CHUNK, SGU_CHUNK), 1)
    weights = [jnp.where(col <= row, w_ref[g], 0.0).astype(jnp.bfloat16) for g in range(SGU_GROUPS)]
    return [jnp.concatenate(weights[2 * sl:2 * sl + 2], axis=1) for sl in range(SGU_WIDTH // LANES)]


def _sgu_rows(u, v, gain, bias, pair_weights, b_mat):
    f32 = jnp.float32
    u = _gelu(u.astype(f32))
    v = _gelu(v.astype(f32))
    mu = jnp.mean(v, axis=-1, keepdims=True)
    vc = v - mu
    v = vc * lax.rsqrt(jnp.mean(vc * vc, axis=-1, keepdims=True) + LN_EPS)
    v = (v * gain + bias).astype(jnp.bfloat16)

    lane = lax.broadcasted_iota(jnp.int32, (SGU_CHUNK, LANES), 1)
    first_group = lane < (SGU_WIDTH // SGU_GROUPS)
    chunks = []
    for n in range(u.shape[0] // SGU_CHUNK):
        rows = slice(n * SGU_CHUNK, (n + 1) * SGU_CHUNK)
        slabs = []
        for sl in range(SGU_WIDTH // LANES):
            cols = slice(sl * LANES, (sl + 1) * LANES)
            vs = v[rows, cols]
            zero = jnp.zeros_like(vs)
            v_diag = jnp.concatenate([jnp.where(first_group, vs, zero), jnp.where(first_group, zero, vs)], axis=0)
            mixed = jnp.dot(pair_weights[sl], v_diag, preferred_element_type=f32) + b_mat[:, cols]
            slabs.append(u[rows, cols] * mixed)
        chunks.append(jnp.concatenate(slabs, axis=1))
    return jnp.concatenate(chunks, axis=0)


def _out_ffn_kernel(x_ref, attn_ref, sgu_ref, ga_ref, gs_ref, wo_ref, gpm_ref, gpf_ref,
                    wg_ref, wu_ref, wd_ref, gpo_ref, o_ref):
    f32 = jnp.float32
    bf16 = jnp.bfloat16
    sub = x_ref.shape[0] // FFN_SUBTILES
    row_slices = [slice(t * sub, (t + 1) * sub) for t in range(FFN_SUBTILES)]

    def out_proj(rows):
        a = _rms(attn_ref[rows, :].astype(f32), ga_ref[...]).astype(bf16)
        s = _rms(sgu_ref[rows, :].astype(f32), gs_ref[...]).astype(bf16)
        return (jnp.dot(a, wo_ref[0:ATTN_WIDTH, :], preferred_element_type=f32)
                + jnp.dot(s, wo_ref[ATTN_WIDTH:, :], preferred_element_type=f32))

    def ffn(rows, y):
        x1 = x_ref[rows, :] + _rms(y, gpm_ref[...])
        h = _rms(x1, gpf_ref[...]).astype(bf16)
        gate = jnp.dot(h, wg_ref[...], preferred_element_type=f32)
        up = jnp.dot(h, wu_ref[...], preferred_element_type=f32)
        act = (gate * jax.nn.sigmoid(gate) * up).astype(bf16)
        return x1, jnp.dot(act, wd_ref[...], preferred_element_type=f32)

    y_next = out_proj(row_slices[0])
    for t, rows in enumerate(row_slices):
        y = y_next
        if t + 1 < FFN_SUBTILES:
            y_next = out_proj(row_slices[t + 1])
        x1, f = ffn(rows, y)
        o_ref[rows, :] = x1 + _rms(f, gpo_ref[...])


def _out_ffn(x2, attn, sgu, ga, gs, wo, gpm, gpf, wg, wu, wd, gpo):
    n_tok = x2.shape[0]
    tm = FFN_TOKEN_TILE

    def const(shape):
        return pl.BlockSpec(shape, lambda i: (0,) * len(shape), pipeline_mode=pl.Buffered(1))

    return pl.pallas_call(
        _out_ffn_kernel,
        grid=(n_tok // tm,),
        in_specs=[
            pl.BlockSpec((tm, D_MODEL), lambda i: (i, 0)),
            pl.BlockSpec((tm, ATTN_WIDTH), lambda i: (i, 0)),
            pl.BlockSpec((tm, SGU_WIDTH), lambda i: (i, 0)),
            const((1, ATTN_WIDTH)),
            const((1, SGU_WIDTH)),
            const((D_MODEL, D_MODEL)),
            const((1, D_MODEL)),
            const((1, D_MODEL)),
            const((D_MODEL, D_FF)),
            const((D_MODEL, D_FF)),
            const((D_FF, D_MODEL)),
            const((1, D_MODEL)),
        ],
        out_specs=pl.BlockSpec((tm, D_MODEL), lambda i: (i, 0)),
        out_shape=jax.ShapeDtypeStruct((n_tok, D_MODEL), jnp.float32),
        compiler_params=pltpu.CompilerParams(
            dimension_semantics=("arbitrary",), vmem_limit_bytes=VMEM_LIMIT_BYTES),
        name="out_ffn",
    )(x2, attn, sgu, ga, gs, wo, gpm, gpf, wg, wu, wd, gpo)


def kernel(x, positions, pre_mix_norm, w_in, sgu_ln_gain, sgu_ln_bias, sgu_w_spatial, sgu_b_spatial, attn_out_norm, sgu_out_norm, w_out, post_mix_norm, pre_ffn_norm, w_gate, w_up, w_down, post_ffn_norm):
    batch, seq, _ = x.shape
    n_tok = batch * seq
    x2 = x.reshape(n_tok, D_MODEL)
    pos_row = positions.reshape(1, n_tok)

    depth = w_in.shape[0]
    for l in range(depth):
        b_mat = jnp.repeat(sgu_b_spatial[l].T, SGU_WIDTH // SGU_GROUPS, axis=1)
        qkv, sgu, (wo, wg, wu, wd) = _in_proj(
            x2, pos_row, pre_mix_norm[l][None, :], w_in[l],
            sgu_ln_gain[l][None, :], sgu_ln_bias[l][None, :], sgu_w_spatial[l], b_mat,
            later_weights=(w_out[l], w_gate[l], w_up[l], w_down[l]))
        attn = _attention(qkv, batch, seq)
        x2 = _out_ffn(x2, attn, sgu, attn_out_norm[l][None, :], sgu_out_norm[l][None, :],
                      wo, post_mix_norm[l][None, :], pre_ffn_norm[l][None, :], wg, wu, wd,
                      post_ffn_norm[l][None, :])
    return x2.reshape(batch, seq, D_MODEL)
```

```python
import functools

import jax
import jax.numpy as jnp
import numpy as np
from jax import lax
from jax.experimental import pallas as pl
from jax.experimental.pallas import tpu as pltpu

D_MODEL = 1024
HEAD_DIM = 64
ATTN_WIDTH = 512
SGU_WIDTH = 512
SGU_GROUPS = 8
SGU_CHUNK = 128
DILATIONS = (1, 4, 16)
WINDOW_STEPS = 128
BLOCK_Q = 128
DILATION_STEP = 4
ATTN_LOOKAHEAD = 2
SCORE_SCALE = float(np.float32(0.125 * np.log2(np.e)))
ROPE_THETA = 500000.0
ROT_DIM = 16
D_FF = 2816
RMS_EPS = 1e-6
LN_EPS = 1e-5
MASK_VALUE = float(np.finfo(np.float32).min)

LANES = 128
BF16_SUBLANE_TILE = 16
VMEM_LIMIT_BYTES = 56 * 1024 * 1024

IN_PROJ_TOKEN_TILE = 1024
IN_PROJ_SUBTILES = 4
FFN_TOKEN_TILE = 1024
FFN_SUBTILES = 4


def _rms(x, gain):
    return x * lax.rsqrt(jnp.mean(x * x, axis=-1, keepdims=True) + RMS_EPS) * gain


def _rotary_tables(pos_row, invf_col, spread):
    f32, bf16 = jnp.float32, jnp.bfloat16
    ang = invf_col * pos_row.astype(f32)
    trig = jnp.concatenate([jnp.cos(ang), jnp.sin(ang)], axis=0)
    hi = trig.astype(bf16)
    rest = trig - hi.astype(f32)
    mid = rest.astype(bf16)
    lo = (rest - mid.astype(f32)).astype(bf16)
    terms = jnp.concatenate([hi, mid, lo], axis=0)
    return lax.dot_general(terms, spread, (((0,), (0,)), ((), ())), preferred_element_type=f32)


def _in_proj_kernel(x_ref, pos_ref, gain_ref, invf_ref, spread_ref, w32_ref,
                    ln_gain_ref, ln_bias_ref, ws_ref, bs_ref, *rest):
    n_cast = len(rest) // 2 - 1
    cast_src, (qkv_ref, sgu_ref), cast_dst, w_ref = (
        rest[:n_cast], rest[n_cast:n_cast + 2], rest[n_cast + 2:2 * n_cast + 2], rest[-1])

    @pl.when(pl.program_id(0) == 0)
    def _():
        w_ref[...] = w32_ref[...].astype(w_ref.dtype)

    for src, dst in zip(cast_src, cast_dst):
        dst[...] = src[...].astype(dst.dtype)

    tables = _rotary_tables(pos_ref[...], invf_ref[...], spread_ref[...])
    pair_weights = _sgu_pair_weights(ws_ref)
    lane = lax.broadcasted_iota(jnp.int32, (1, LANES), 1)
    ones_off_rotary = jnp.where((lane % HEAD_DIM) < ROT_DIM, 0.0, 1.0)
    half = ROT_DIM // 2
    q_slabs = ATTN_WIDTH // LANES
    qk_width = 2 * ATTN_WIDTH
    uv_lo = 3 * ATTN_WIDTH

    sub = x_ref.shape[0] // IN_PROJ_SUBTILES
    row_slices = [slice(t * sub, (t + 1) * sub) for t in range(IN_PROJ_SUBTILES)]

    def project(rows):
        h = _rms(x_ref[rows, :], gain_ref[...]).astype(jnp.bfloat16)
        return jnp.dot(h, w_ref[...], preferred_element_type=jnp.float32)

    proj_next = project(row_slices[0])
    for t, rows in enumerate(row_slices):
        proj = proj_next
        if t + 1 < IN_PROJ_SUBTILES:
            proj_next = project(row_slices[t + 1])
        cos = tables[rows, :LANES] + ones_off_rotary
        sin_from_below = tables[rows, LANES:2 * LANES]
        sin_from_above = tables[rows, 2 * LANES:]
        for j in range(2 * q_slabs):
            slab = proj[:, j * LANES:(j + 1) * LANES]
            rot = (slab * cos
                   + pltpu.roll(slab, half, axis=1) * sin_from_below
                   + pltpu.roll(slab, LANES - half, axis=1) * sin_from_above)
            if j < q_slabs:
                rot = rot * SCORE_SCALE
            qkv_ref[rows, j * LANES:(j + 1) * LANES] = rot.astype(qkv_ref.dtype)
        qkv_ref[rows, qk_width:] = proj[:, qk_width:uv_lo].astype(qkv_ref.dtype)
        sgu = _sgu_rows(proj[:, uv_lo:uv_lo + SGU_WIDTH], proj[:, uv_lo + SGU_WIDTH:],
                        ln_gain_ref[...], ln_bias_ref[...], pair_weights, bs_ref[...])
        sgu_ref[rows, :] = sgu.astype(sgu_ref.dtype)


def _rotary_constants():
    n_freq = ROT_DIM // 2
    inv_freq = ROPE_THETA ** (-jnp.arange(0, ROT_DIM, 2, dtype=jnp.float32) / ROT_DIM)
    lane = np.arange(LANES)
    in_rot = (lane % HEAD_DIM) < ROT_DIM
    upper = (lane % ROT_DIM) >= n_freq
    hit = (lane[None, :] % n_freq) == np.arange(n_freq)[:, None]
    cos_rows = np.concatenate([hit & in_rot, np.zeros_like(hit), np.zeros_like(hit)], axis=1)
    sin_rows = np.concatenate([np.zeros(hit.shape), 1.0 * (hit & in_rot & upper),
                               -1.0 * (hit & in_rot & ~upper)], axis=1)
    one_term = np.concatenate([cos_rows.astype(np.float32), sin_rows.astype(np.float32)], axis=0)
    spread = np.concatenate([one_term] * 3, axis=0)
    return inv_freq.reshape(n_freq, 1), jnp.asarray(spread, dtype=jnp.bfloat16)


def _in_proj(x2, pos_row, gain, w_in, ln_gain, ln_bias, w_spatial, b_mat, later_weights):
    n_tok = x2.shape[0]
    tm = IN_PROJ_TOKEN_TILE
    n_steps = n_tok // tm
    width = w_in.shape[1]
    bf16 = jnp.bfloat16
    invf, spread = _rotary_constants()
    assert (tm // IN_PROJ_SUBTILES) % SGU_CHUNK == 0

    def const(shape, **kwargs):
        return pl.BlockSpec(shape, lambda i: (0,) * len(shape), **kwargs)

    def row_slab(w):
        rows = w.shape[0] // n_steps
        assert rows * n_steps == w.shape[0] and rows % BF16_SUBLANE_TILE == 0
        return pl.BlockSpec((rows, w.shape[1]), lambda i: (i, 0))

    slabs = [row_slab(w) for w in later_weights]
    outs = pl.pallas_call(
        _in_proj_kernel,
        grid=(n_steps,),
        in_specs=[
            pl.BlockSpec((tm, D_MODEL), lambda i: (i, 0)),
            pl.BlockSpec((1, tm), lambda i: (0, i)),
            const((1, D_MODEL)),
            const(invf.shape),
            const(spread.shape),
            const((D_MODEL, width), pipeline_mode=pl.Buffered(1)),
            const((1, SGU_WIDTH)),
            const((1, SGU_WIDTH)),
            const((SGU_GROUPS, SGU_CHUNK, SGU_CHUNK)),
            const((SGU_CHUNK, SGU_WIDTH)),
        ] + slabs,
        out_specs=[
            pl.BlockSpec((tm, 3 * ATTN_WIDTH), lambda i: (i, 0)),
            pl.BlockSpec((tm, SGU_WIDTH), lambda i: (i, 0)),
        ] + slabs,
        out_shape=[
            jax.ShapeDtypeStruct((n_tok, 3 * ATTN_WIDTH), bf16),
            jax.ShapeDtypeStruct((n_tok, SGU_WIDTH), bf16),
        ] + [jax.ShapeDtypeStruct(w.shape, bf16) for w in later_weights],
        scratch_shapes=[pltpu.VMEM((D_MODEL, width), bf16)],
        compiler_params=pltpu.CompilerParams(
            dimension_semantics=("arbitrary",), vmem_limit_bytes=VMEM_LIMIT_BYTES),
        name="in_proj",
    )(x2, pos_row, gain, invf, spread, w_in, ln_gain, ln_bias, w_spatial, b_mat, *later_weights)
    return outs[0], outs[1], outs[2:]


def _attn_kernel(q_ref, k_ref, v_ref, o_ref,
                 q32, k32, v32, q32g, k32g, v32g, qp1, kp1, vp1, qp2, kp2, vp2,
                 bias_ref, ot, mt, dt, out32, *, seq):
    n_blocks = seq // BLOCK_Q
    n_pat = len(DILATIONS)
    f32 = jnp.float32
    bf16 = jnp.bfloat16

    row = lax.broadcasted_iota(jnp.int32, (BLOCK_Q, 2 * BLOCK_Q), 0)
    col = lax.broadcasted_iota(jnp.int32, (BLOCK_Q, 2 * BLOCK_Q), 1)
    keep = (col >= row) & (col <= row + WINDOW_STEPS)
    bias_ref[...] = jnp.where(keep, 0.0, MASK_VALUE)

    lane = lax.broadcasted_iota(jnp.int32, (BLOCK_Q, LANES), 1)
    first_head = lane < HEAD_DIM

    def head_select(a, b):
        return jnp.where(first_head, a, b)

    head_ones = [jnp.where(first_head, 1.0, 0.0).astype(bf16), jnp.where(first_head, 0.0, 1.0).astype(bf16)]

    def regroup(src, chunk, dst32, dst16):
        quarter = chunk // DILATION_STEP
        for c in range(seq // chunk):
            for r in range(DILATION_STEP):
                rows = src[pl.ds(c * chunk + r, quarter, stride=DILATION_STEP), :]
                lo = c * chunk + r * quarter
                if dst32 is not None:
                    dst32[lo:lo + quarter, :] = rows
                dst16[lo:lo + quarter, :] = rows.astype(bf16)

    sources = [(q_ref, k_ref, v_ref), (qp1, kp1, vp1), (qp2, kp2, vp2)]
    for x_ref, x32, x32g, xp1, xp2 in ((q_ref, q32, q32g, qp1, qp2), (k_ref, k32, k32g, kp1, kp2),
                                       (v_ref, v32, v32g, vp1, vp2)):
        x32[...] = x_ref[...].astype(f32)
        regroup(x32, seq, x32g, xp1)
        regroup(x32g, seq // DILATION_STEP, None, xp2)

    def level1_rows(level, i):
        if level < 2:
            return pl.ds(i * BLOCK_Q, BLOCK_Q)
        c, r = divmod(i, DILATION_STEP)
        return pl.ds(c * DILATION_STEP * BLOCK_Q + r, BLOCK_Q, stride=DILATION_STEP)

    def key_window(level, i):
        blocks_per_sub = n_blocks // DILATIONS[level]
        key_lo = i * BLOCK_Q if i % blocks_per_sub == 0 else (i - 1) * BLOCK_Q
        return key_lo, (i + 1) * BLOCK_Q - key_lo

    def probabilities(level, i):
        q_src, k_src, _ = sources[level]
        key_lo, n_keys = key_window(level, i)
        qb = q_src[i * BLOCK_Q:(i + 1) * BLOCK_Q, :]
        kw = k_src[key_lo:key_lo + n_keys, :]
        bias = bias_ref[:, 2 * BLOCK_Q - n_keys:]
        zero = jnp.zeros_like(qb)
        if n_keys == BLOCK_Q:
            k2 = jnp.concatenate([jnp.where(first_head, kw, zero), jnp.where(first_head, zero, kw)], axis=0)
            s = lax.dot_general(qb, k2, (((1,), (1,)), ((), ())), preferred_element_type=f32)
            halves = [s[:, h * BLOCK_Q:(h + 1) * BLOCK_Q] + bias for h in range(2)]
            maxes = [jnp.max(x, axis=-1, keepdims=True) for x in halves]
            e = jnp.concatenate([jnp.exp2(x - m) for x, m in zip(halves, maxes)], axis=1)
            m_tile = head_select(*[jnp.broadcast_to(m, (BLOCK_Q, LANES)) for m in maxes])
            return e.astype(bf16), m_tile
        q2 = jnp.concatenate([jnp.where(first_head, qb, zero), jnp.where(first_head, zero, qb)], axis=0)
        s = lax.dot_general(q2, kw, (((1,), (1,)), ((), ())), preferred_element_type=f32)
        s = s + jnp.concatenate([bias, bias], axis=0)
        m = jnp.max(s, axis=-1, keepdims=True)
        m_wide = jnp.broadcast_to(m, (2 * BLOCK_Q, LANES))
        m_tile = head_select(m_wide[:BLOCK_Q], m_wide[BLOCK_Q:])
        return jnp.exp2(s - m).astype(bf16), m_tile

    def weighted_values(level, i, e, m_tile):
        key_lo, n_keys = key_window(level, i)
        vw = sources[level][2][key_lo:key_lo + n_keys, :]
        dst = level1_rows(level, i)
        mt[level, dst, :] = m_tile
        if n_keys == BLOCK_Q:
            zero = jnp.zeros_like(vw)
            v2 = jnp.concatenate([
                jnp.concatenate([jnp.where(first_head, vw, zero), head_ones[0]], axis=1),
                jnp.concatenate([jnp.where(first_head, zero, vw), head_ones[1]], axis=1)], axis=0)
            o2 = jnp.dot(e, v2, preferred_element_type=f32)
            ot[level, dst, :] = o2[:, :LANES]
            dt[level, dst, :] = o2[:, LANES:]
            return
        v_ext = jnp.concatenate([vw, jnp.ones((n_keys, LANES), bf16)], axis=1)
        o2 = jnp.dot(e, v_ext, preferred_element_type=f32)
        ot[level, dst, :] = head_select(o2[:BLOCK_Q, :LANES], o2[BLOCK_Q:, :LANES])
        dt[level, dst, :] = head_select(o2[:BLOCK_Q, LANES:], o2[BLOCK_Q:, LANES:])

    quarter_blocks = n_blocks // DILATION_STEP

    def merge(b):
        r, j = divmod(b, quarter_blocks)
        rows0 = pl.ds(r + DILATION_STEP * BLOCK_Q * j, BLOCK_Q, stride=DILATION_STEP)
        rows1 = pl.ds(b * BLOCK_Q, BLOCK_Q)
        rows = [rows0] + [rows1] * (n_pat - 1)
        maxes = [mt[p, rows[p], :] for p in range(n_pat)]
        m_all = functools.reduce(jnp.maximum, maxes)
        num = jnp.zeros((BLOCK_Q, LANES), f32)
        den = jnp.zeros((BLOCK_Q, LANES), f32)
        for p in range(n_pat):
            w = jnp.exp2(maxes[p] - m_all)
            num = num + w * ot[p, rows[p], :]
            den = den + w * dt[p, rows[p], :]
        out32[rows0, :] = num / den

    head = n_blocks // 2
    order = [(0, i) for i in range(head)]
    for i in range(n_blocks - head):
        order += [(0, head + i), (2, 2 * i), (2, 2 * i + 1)]
    order += [(1, b) for b in range(n_blocks)]
    pending = {}
    for n in range(len(order) + ATTN_LOOKAHEAD):
        if n < len(order):
            pending[order[n]] = probabilities(*order[n])
        if n >= ATTN_LOOKAHEAD:
            level, i = order[n - ATTN_LOOKAHEAD]
            weighted_values(level, i, *pending.pop((level, i)))
            if level == 1:
                merge(i)
    o_ref[...] = out32[...].astype(o_ref.dtype)


def _attention(qkv, batch, seq):
    n_tok = qkv.shape[0]
    n_pairs = ATTN_WIDTH // LANES
    f32, bf16 = jnp.float32, jnp.bfloat16
    n_pat = len(DILATIONS)
    assert all(d == DILATION_STEP ** n for n, d in enumerate(DILATIONS)) and n_pat == 3
    assert seq % (DILATIONS[-1] * BLOCK_Q) == 0
    return pl.pallas_call(
        functools.partial(_attn_kernel, seq=seq),
        grid=(batch, n_pairs),
        in_specs=[
            pl.BlockSpec((seq, LANES), lambda b, h: (b, h)),
            pl.BlockSpec((seq, LANES), lambda b, h: (b, n_pairs + h)),
            pl.BlockSpec((seq, LANES), lambda b, h: (b, 2 * n_pairs + h)),
        ],
        out_specs=pl.BlockSpec((seq, LANES), lambda b, h: (b, h)),
        out_shape=jax.ShapeDtypeStruct((n_tok, ATTN_WIDTH), bf16),
        scratch_shapes=(
            [pltpu.VMEM((seq, LANES), f32)] * 6
            + [pltpu.VMEM((seq, LANES), bf16)] * 6
            + [pltpu.VMEM((BLOCK_Q, 2 * BLOCK_Q), f32)]
            + [pltpu.VMEM((n_pat, seq, LANES), f32)] * 3
            + [pltpu.VMEM((seq, LANES), f32)]
        ),
        compiler_params=pltpu.CompilerParams(
            dimension_semantics=("arbitrary", "arbitrary"), vmem_limit_bytes=VMEM_LIMIT_BYTES),
        name="attention",
    )(qkv, qkv, qkv)


def _gelu(x):
    return 0.5 * x * (1.0 + lax.erf(x * np.float32(1.0 / np.sqrt(2.0))))


def _sgu_pair_weights(w_ref):
    row = lax.broadcasted_iota(jnp.int32, (SGU_CHUNK, SGU_CHUNK), 0)
    col = lax.broadcasted_iota(jnp.int32, (SGU_CHUNK, SGU_CHUNK), 1)
    weights = [jnp.where(col <= row, w_ref[g], 0.0).astype(jnp.bfloat16) for g in range(SGU_GROUPS)]
    return [jnp.concatenate(weights[2 * sl:2 * sl + 2], axis=1) for sl in range(SGU_WIDTH // LANES)]


def _sgu_rows(u, v, gain, bias, pair_weights, b_mat):
    f32 = jnp.float32
    u = _gelu(u.astype(f32))
    v = _gelu(v.astype(f32))
    mu = jnp.mean(v, axis=-1, keepdims=True)
    vc = v - mu
    v = vc * lax.rsqrt(jnp.mean(vc * vc, axis=-1, keepdims=True) + LN_EPS)
    v = (v * gain + bias).astype(jnp.bfloat16)

    lane = lax.broadcasted_iota(jnp.int32, (SGU_CHUNK, LANES), 1)
    first_group = lane < (SGU_WIDTH // SGU_GROUPS)
    chunks = []
    for n in range(u.shape[0] // SGU_CHUNK):
        rows = slice(n * SGU_CHUNK, (n + 1) * SGU_CHUNK)
        slabs = []
        for sl in range(SGU_WIDTH // LANES):
            cols = slice(sl * LANES, (sl + 1) * LANES)
            vs = v[rows, cols]
            zero = jnp.zeros_like(vs)
            v_diag = jnp.concatenate([jnp.where(first_group, vs, zero), jnp.where(first_group, zero, vs)], axis=0)
            mixed = jnp.dot(pair_weights[sl], v_diag, preferred_element_type=f32) + b_mat[:, cols]
            slabs.append(u[rows, cols] * mixed)
        chunks.append(jnp.concatenate(slabs, axis=1))
    return jnp.concatenate(chunks, axis=0)


def _out_ffn_kernel(x_ref, attn_ref, sgu_ref, ga_ref, gs_ref, wo_ref, gpm_ref, gpf_ref,
                    wg_ref, wu_ref, wd_ref, gpo_ref, o_ref):
    f32 = jnp.float32
    bf16 = jnp.bfloat16
    sub = x_ref.shape[0] // FFN_SUBTILES
    row_slices = [slice(t * sub, (t + 1) * sub) for t in range(FFN_SUBTILES)]

    def out_proj(rows):
        a = _rms(attn_ref[rows, :].astype(f32), ga_ref[...]).astype(bf16)
        s = _rms(sgu_ref[rows, :].astype(f32), gs_ref[...]).astype(bf16)
        return (jnp.dot(a, wo_ref[0:ATTN_WIDTH, :], preferred_element_type=f32)
                + jnp.dot(s, wo_ref[ATTN_WIDTH:, :], preferred_element_type=f32))

    def ffn(rows, y):
        x1 = x_ref[rows, :] + _rms(y, gpm_ref[...])
        h = _rms(x1, gpf_ref[...]).astype(bf16)
        gate = jnp.dot(h, wg_ref[...], preferred_element_type=f32)
        up = jnp.dot(h, wu_ref[...], preferred_element_type=f32)
        act = (gate * jax.nn.sigmoid(gate) * up).astype(bf16)
        return x1, jnp.dot(act, wd_ref[...], preferred_element_type=f32)

    y_next = out_proj(row_slices[0])
    for t, rows in enumerate(row_slices):
        y = y_next
        if t + 1 < FFN_SUBTILES:
            y_next = out_proj(row_slices[t + 1])
        x1, f = ffn(rows, y)
        o_ref[rows, :] = x1 + _rms(f, gpo_ref[...])


def _out_ffn(x2, attn, sgu, ga, gs, wo, gpm, gpf, wg, wu, wd, gpo):
    n_tok = x2.shape[0]
    tm = FFN_TOKEN_TILE

    def const(shape):
        return pl.BlockSpec(shape, lambda i: (0,) * len(shape), pipeline_mode=pl.Buffered(1))

    return pl.pallas_call(
        _out_ffn_kernel,
        grid=(n_tok // tm,),
        in_specs=[
            pl.BlockSpec((tm, D_MODEL), lambda i: (i, 0)),
            pl.BlockSpec((tm, ATTN_WIDTH), lambda i: (i, 0)),
            pl.BlockSpec((tm, SGU_WIDTH), lambda i: (i, 0)),
            const((1, ATTN_WIDTH)),
            const((1, SGU_WIDTH)),
            const((D_MODEL, D_MODEL)),
            const((1, D_MODEL)),
            const((1, D_MODEL)),
            const((D_MODEL, D_FF)),
            const((D_MODEL, D_FF)),
            const((D_FF, D_MODEL)),
            const((1, D_MODEL)),
        ],
        out_specs=pl.BlockSpec((tm, D_MODEL), lambda i: (i, 0)),
        out_shape=jax.ShapeDtypeStruct((n_tok, D_MODEL), jnp.float32),
        compiler_params=pltpu.CompilerParams(
            dimension_semantics=("arbitrary",), vmem_limit_bytes=VMEM_LIMIT_BYTES),
        name="out_ffn",
    )(x2, attn, sgu, ga, gs, wo, gpm, gpf, wg, wu, wd, gpo)


def kernel(x, positions, pre_mix_norm, w_in, sgu_ln_gain, sgu_ln_bias, sgu_w_spatial, sgu_b_spatial, attn_out_norm, sgu_out_norm, w_out, post_mix_norm, pre_ffn_norm, w_gate, w_up, w_down, post_ffn_norm):
    batch, seq, _ = x.shape
    n_tok = batch * seq
    x2 = x.reshape(n_tok, D_MODEL)
    pos_row = positions.reshape(1, n_tok)

    depth = w_in.shape[0]
    for l in range(depth):
        b_mat = jnp.repeat(sgu_b_spatial[l].T, SGU_WIDTH // SGU_GROUPS, axis=1)
        qkv, sgu, (wo, wg, wu, wd) = _in_proj(
            x2, pos_row, pre_mix_norm[l][None, :], w_in[l],
            sgu_ln_gain[l][None, :], sgu_ln_bias[l][None, :], sgu_w_spatial[l], b_mat,
            later_weights=(w_out[l], w_gate[l], w_up[l], w_down[l]))
        attn = _attention(qkv, batch, seq)
        x2 = _out_ffn(x2, attn, sgu, attn_out_norm[l][None, :], sgu_out_norm[l][None, :],
                      wo, post_mix_norm[l][None, :], pre_ffn_norm[l][None, :], wg, wu, wd,
                      post_ffn_norm[l][None, :])
    return x2.reshape(batch, seq, D_MODEL)
```

```python
import functools

import jax
import jax.numpy as jnp
import numpy as np
from jax import lax
from jax.experimental import pallas as pl
from jax.experimental.pallas import tpu as pltpu

D_MODEL = 1024
HEAD_DIM = 64
ATTN_WIDTH = 512
SGU_WIDTH = 512
SGU_GROUPS = 8
SGU_CHUNK = 128
DILATIONS = (1, 4, 16)
WINDOW_STEPS = 128
BLOCK_Q = 128
DILATION_STEP = 4
ATTN_LOOKAHEAD = 2
SCORE_SCALE = float(np.float32(0.125 * np.log2(np.e)))
ROPE_THETA = 500000.0
ROT_DIM = 16
D_FF = 2816
RMS_EPS = 1e-6
LN_EPS = 1e-5
MASK_VALUE = float(np.finfo(np.float32).min)

LANES = 128
BF16_SUBLANE_TILE = 16
VMEM_LIMIT_BYTES = 56 * 1024 * 1024

IN_PROJ_TOKEN_TILE = 1024
IN_PROJ_SUBTILES = 4
FFN_TOKEN_TILE = 1024
FFN_SUBTILES = 4


def _rms(x, gain):
    return x * lax.rsqrt(jnp.mean(x * x, axis=-1, keepdims=True) + RMS_EPS) * gain


def _rotary_tables(pos_row, invf_col, spread):
    f32, bf16 = jnp.float32, jnp.bfloat16
    ang = invf_col * pos_row.astype(f32)
    trig = jnp.concatenate([jnp.cos(ang), jnp.sin(ang)], axis=0)
    hi = trig.astype(bf16)
    rest = trig - hi.astype(f32)
    mid = rest.astype(bf16)
    lo = (rest - mid.astype(f32)).astype(bf16)
    terms = jnp.concatenate([hi, mid, lo], axis=0)
    return lax.dot_general(terms, spread, (((0,), (0,)), ((), ())), preferred_element_type=f32)


def _in_proj_kernel(x_ref, pos_ref, gain_ref, invf_ref, spread_ref, w32_ref,
                    ln_gain_ref, ln_bias_ref, ws_ref, bs_ref, *rest):
    n_cast = len(rest) // 2 - 1
    cast_src, (qkv_ref, sgu_ref), cast_dst, w_ref = (
        rest[:n_cast], rest[n_cast:n_cast + 2], rest[n_cast + 2:2 * n_cast + 2], rest[-1])

    @pl.when(pl.program_id(0) == 0)
    def _():
        w_ref[...] = w32_ref[...].astype(w_ref.dtype)

    for src, dst in zip(cast_src, cast_dst):
        dst[...] = src[...].astype(dst.dtype)

    tables = _rotary_tables(pos_ref[...], invf_ref[...], spread_ref[...])
    pair_weights = _sgu_pair_weights(ws_ref)
    lane = lax.broadcasted_iota(jnp.int32, (1, LANES), 1)
    ones_off_rotary = jnp.where((lane % HEAD_DIM) < ROT_DIM, 0.0, 1.0)
    half = ROT_DIM // 2
    q_slabs = ATTN_WIDTH // LANES
    qk_width = 2 * ATTN_WIDTH
    uv_lo = 3 * ATTN_WIDTH

    sub = x_ref.shape[0] // IN_PROJ_SUBTILES
    row_slices = [slice(t * sub, (t + 1) * sub) for t in range(IN_PROJ_SUBTILES)]

    def project(rows):
        h = _rms(x_ref[rows, :], gain_ref[...]).astype(jnp.bfloat16)
        return jnp.dot(h, w_ref[...], preferred_element_type=jnp.float32)

    proj_next = project(row_slices[0])
    for t, rows in enumerate(row_slices):
        proj = proj_next
        if t + 1 < IN_PROJ_SUBTILES:
            proj_next = project(row_slices[t + 1])
        cos = tables[rows, :LANES] + ones_off_rotary
        sin_from_below = tables[rows, LANES:2 * LANES]
        sin_from_above = tables[rows, 2 * LANES:]
        for j in range(2 * q_slabs):
            slab = proj[:, j * LANES:(j + 1) * LANES]
            rot = (slab * cos
                   + pltpu.roll(slab, half, axis=1) * sin_from_below
                   + pltpu.roll(slab, LANES - half, axis=1) * sin_from_above)
            if j < q_slabs:
                rot = rot * SCORE_SCALE
            qkv_ref[rows, j * LANES:(j + 1) * LANES] = rot.astype(qkv_ref.dtype)
        qkv_ref[rows, qk_width:] = proj[:, qk_width:uv_lo].astype(qkv_ref.dtype)
        sgu = _sgu_rows(proj[:, uv_lo:uv_lo + SGU_WIDTH], proj[:, uv_lo + SGU_WIDTH:],
                        ln_gain_ref[...], ln_bias_ref[...], pair_weights, bs_ref[...])
        sgu_ref[rows, :] = sgu.astype(sgu_ref.dtype)


def _rotary_constants():
    n_freq = ROT_DIM // 2
    inv_freq = ROPE_THETA ** (-jnp.arange(0, ROT_DIM, 2, dtype=jnp.float32) / ROT_DIM)
    lane = np.arange(LANES)
    in_rot = (lane % HEAD_DIM) < ROT_DIM
    upper = (lane % ROT_DIM) >= n_freq
    hit = (lane[None, :] % n_freq) == np.arange(n_freq)[:, None]
    cos_rows = np.concatenate([hit & in_rot, np.zeros_like(hit), np.zeros_like(hit)], axis=1)
    sin_rows = np.concatenate([np.zeros(hit.shape), 1.0 * (hit & in_rot & upper),
                               -1.0 * (hit & in_rot & ~upper)], axis=1)
    one_term = np.concatenate([cos_rows.astype(np.float32), sin_rows.astype(np.float32)], axis=0)
    spread = np.concatenate([one_term] * 3, axis=0)
    return inv_freq.reshape(n_freq, 1), jnp.asarray(spread, dtype=jnp.bfloat16)


def _in_proj(x2, pos_row, gain, w_in, ln_gain, ln_bias, w_spatial, b_mat, later_weights):
    n_tok = x2.shape[0]
    tm = IN_PROJ_TOKEN_TILE
    n_steps = n_tok // tm
    width = w_in.shape[1]
    bf16 = jnp.bfloat16
    invf, spread = _rotary_constants()
    assert (tm // IN_PROJ_SUBTILES) % SGU_CHUNK == 0

    def const(shape, **kwargs):
        return pl.BlockSpec(shape, lambda i: (0,) * len(shape), **kwargs)

    def row_slab(w):
        rows = w.shape[0] // n_steps
        assert rows * n_steps == w.shape[0] and rows % BF16_SUBLANE_TILE == 0
        return pl.BlockSpec((rows, w.shape[1]), lambda i: (i, 0))

    slabs = [row_slab(w) for w in later_weights]
    outs = pl.pallas_call(
        _in_proj_kernel,
        grid=(n_steps,),
        in_specs=[
            pl.BlockSpec((tm, D_MODEL), lambda i: (i, 0)),
            pl.BlockSpec((1, tm), lambda i: (0, i)),
            const((1, D_MODEL)),
            const(invf.shape),
            const(spread.shape),
            const((D_MODEL, width), pipeline_mode=pl.Buffered(1)),
            const((1, SGU_WIDTH)),
            const((1, SGU_WIDTH)),
            const((SGU_GROUPS, SGU_CHUNK, SGU_CHUNK)),
            const((SGU_CHUNK, SGU_WIDTH)),
        ] + slabs,
        out_specs=[
            pl.BlockSpec((tm, 3 * ATTN_WIDTH), lambda i: (i, 0)),
            pl.BlockSpec((tm, SGU_WIDTH), lambda i: (i, 0)),
        ] + slabs,
        out_shape=[
            jax.ShapeDtypeStruct((n_tok, 3 * ATTN_WIDTH), bf16),
            jax.ShapeDtypeStruct((n_tok, SGU_WIDTH), bf16),
        ] + [jax.ShapeDtypeStruct(w.shape, bf16) for w in later_weights],
        scratch_shapes=[pltpu.VMEM((D_MODEL, width), bf16)],
        compiler_params=pltpu.CompilerParams(
            dimension_semantics=("arbitrary",), vmem_limit_bytes=VMEM_LIMIT_BYTES),
        name="in_proj",
    )(x2, pos_row, gain, invf, spread, w_in, ln_gain, ln_bias, w_spatial, b_mat, *later_weights)
    return outs[0], outs[1], outs[2:]


def _attn_kernel(q_ref, k_ref, v_ref, o_ref,
                 q32, k32, v32, q32g, k32g, v32g, qp1, kp1, vp1, qp2, kp2, vp2,
                 bias_ref, ot, mt, dt, out32, *, seq):
    n_blocks = seq // BLOCK_Q
    n_pat = len(DILATIONS)
    f32 = jnp.float32
    bf16 = jnp.bfloat16

    row = lax.broadcasted_iota(jnp.int32, (BLOCK_Q, 2 * BLOCK_Q), 0)
    col = lax.broadcasted_iota(jnp.int32, (BLOCK_Q, 2 * BLOCK_Q), 1)
    keep = (col >= row) & (col <= row + WINDOW_STEPS)
    bias_ref[...] = jnp.where(keep, 0.0, MASK_VALUE)

    lane = lax.broadcasted_iota(jnp.int32, (BLOCK_Q, LANES), 1)
    first_head = lane < HEAD_DIM

    def head_select(a, b):
        return jnp.where(first_head, a, b)

    head_ones = [jnp.where(first_head, 1.0, 0.0).astype(bf16), jnp.where(first_head, 0.0, 1.0).astype(bf16)]

    def regroup(src, chunk, dst32, dst16):
        quarter = chunk // DILATION_STEP
        for c in range(seq // chunk):
            for r in range(DILATION_STEP):
                rows = src[pl.ds(c * chunk + r, quarter, stride=DILATION_STEP), :]
                lo = c * chunk + r * quarter
                if dst32 is not None:
                    dst32[lo:lo + quarter, :] = rows
                dst16[lo:lo + quarter, :] = rows.astype(bf16)

    sources = [(q_ref, k_ref, v_ref), (qp1, kp1, vp1), (qp2, kp2, vp2)]
    for x_ref, x32, x32g, xp1, xp2 in ((q_ref, q32, q32g, qp1, qp2), (k_ref, k32, k32g, kp1, kp2),
                                       (v_ref, v32, v32g, vp1, vp2)):
        x32[...] = x_ref[...].astype(f32)
        regroup(x32, seq, x32g, xp1)
        regroup(x32g, seq // DILATION_STEP, None, xp2)

    def level1_rows(level, i):
        if level < 2:
            return pl.ds(i * BLOCK_Q, BLOCK_Q)
        c, r = divmod(i, DILATION_STEP)
        return pl.ds(c * DILATION_STEP * BLOCK_Q + r, BLOCK_Q, stride=DILATION_STEP)

    def key_window(level, i):
        blocks_per_sub = n_blocks // DILATIONS[level]
        key_lo = i * BLOCK_Q if i % blocks_per_sub == 0 else (i - 1) * BLOCK_Q
        return key_lo, (i + 1) * BLOCK_Q - key_lo

    def probabilities(level, i):
        q_src, k_src, _ = sources[level]
        key_lo, n_keys = key_window(level, i)
        qb = q_src[i * BLOCK_Q:(i + 1) * BLOCK_Q, :]
        kw = k_src[key_lo:key_lo + n_keys, :]
        bias = bias_ref[:, 2 * BLOCK_Q - n_keys:]
        zero = jnp.zeros_like(qb)
        if n_keys == BLOCK_Q:
            k2 = jnp.concatenate([jnp.where(first_head, kw, zero), jnp.where(first_head, zero, kw)], axis=0)
            s = lax.dot_general(qb, k2, (((1,), (1,)), ((), ())), preferred_element_type=f32)
            halves = [s[:, h * BLOCK_Q:(h + 1) * BLOCK_Q] + bias for h in range(2)]
            maxes = [jnp.max(x, axis=-1, keepdims=True) for x in halves]
            e = jnp.concatenate([jnp.exp2(x - m) for x, m in zip(halves, maxes)], axis=1)
            m_tile = head_select(*[jnp.broadcast_to(m, (BLOCK_Q, LANES)) for m in maxes])
            return e.astype(bf16), m_tile
        q2 = jnp.concatenate([jnp.where(first_head, qb, zero), jnp.where(first_head, zero, qb)], axis=0)
        s = lax.dot_general(q2, kw, (((1,), (1,)), ((), ())), preferred_element_type=f32)
        s = s + jnp.concatenate([bias, bias], axis=0)
        m = jnp.max(s, axis=-1, keepdims=True)
        m_wide = jnp.broadcast_to(m, (2 * BLOCK_Q, LANES))
        m_tile = head_select(m_wide[:BLOCK_Q], m_wide[BLOCK_Q:])
        return jnp.exp2(s - m).astype(bf16), m_tile

    def weighted_values(level, i, e, m_tile):
        key_lo, n_keys = key_window(level, i)
        vw = sources[level][2][key_lo:key_lo + n_keys, :]
        dst = level1_rows(level, i)
        mt[level, dst, :] = m_tile
        if n_keys == BLOCK_Q:
            zero = jnp.zeros_like(vw)
            v2 = jnp.concatenate([
                jnp.concatenate([jnp.where(first_head, vw, zero), head_ones[0]], axis=1),
                jnp.concatenate([jnp.where(first_head, zero, vw), head_ones[1]], axis=1)], axis=0)
            o2 = jnp.dot(e, v2, preferred_element_type=f32)
            ot[level, dst, :] = o2[:, :LANES]
            dt[level, dst, :] = o2[:, LANES:]
            return
        v_ext = jnp.concatenate([vw, jnp.ones((n_keys, LANES), bf16)], axis=1)
        o2 = jnp.dot(e, v_ext, preferred_element_type=f32)
        ot[level, dst, :] = head_select(o2[:BLOCK_Q, :LANES], o2[BLOCK_Q:, :LANES])
        dt[level, dst, :] = head_select(o2[:BLOCK_Q, LANES:], o2[BLOCK_Q:, LANES:])

    quarter_blocks = n_blocks // DILATION_STEP

    def merge(b):
        r, j = divmod(b, quarter_blocks)
        rows0 = pl.ds(r + DILATION_STEP * BLOCK_Q * j, BLOCK_Q, stride=DILATION_STEP)
        rows1 = pl.ds(b * BLOCK_Q, BLOCK_Q)
        rows = [rows0] + [rows1] * (n_pat - 1)
        maxes = [mt[p, rows[p], :] for p in range(n_pat)]
        m_all = functools.reduce(jnp.maximum, maxes)
        num = jnp.zeros((BLOCK_Q, LANES), f32)
        den = jnp.zeros((BLOCK_Q, LANES), f32)
        for p in range(n_pat):
            w = jnp.exp2(maxes[p] - m_all)
            num = num + w * ot[p, rows[p], :]
            den = den + w * dt[p, rows[p], :]
        out32[rows0, :] = num / den

    head = n_blocks // 4
    order = [(0, i) for i in range(head)]
    for i in range(n_blocks - head):
        order += [(0, head + i)] + [(2, j) for j in range(n_blocks) if j * (n_blocks - head) // n_blocks == i]
    order += [(1, b) for b in range(n_blocks)]
    pending = {}
    for n in range(len(order) + ATTN_LOOKAHEAD):
        if n < len(order):
            pending[order[n]] = probabilities(*order[n])
        if n >= ATTN_LOOKAHEAD:
            level, i = order[n - ATTN_LOOKAHEAD]
            weighted_values(level, i, *pending.pop((level, i)))
            if level == 1:
                merge(i)
    o_ref[...] = out32[...].astype(o_ref.dtype)


def _attention(qkv, batch, seq):
    n_tok = qkv.shape[0]
    n_pairs = ATTN_WIDTH // LANES
    f32, bf16 = jnp.float32, jnp.bfloat16
    n_pat = len(DILATIONS)
    assert all(d == DILATION_STEP ** n for n, d in enumerate(DILATIONS)) and n_pat == 3
    assert seq % (DILATIONS[-1] * BLOCK_Q) == 0
    return pl.pallas_call(
        functools.partial(_attn_kernel, seq=seq),
        grid=(batch, n_pairs),
        in_specs=[
            pl.BlockSpec((seq, LANES), lambda b, h: (b, h)),
            pl.BlockSpec((seq, LANES), lambda b, h: (b, n_pairs + h)),
            pl.BlockSpec((seq, LANES), lambda b, h: (b, 2 * n_pairs + h)),
        ],
        out_specs=pl.BlockSpec((seq, LANES), lambda b, h: (b, h)),
        out_shape=jax.ShapeDtypeStruct((n_tok, ATTN_WIDTH), bf16),
        scratch_shapes=(
            [pltpu.VMEM((seq, LANES), f32)] * 6
            + [pltpu.VMEM((seq, LANES), bf16)] * 6
            + [pltpu.VMEM((BLOCK_Q, 2 * BLOCK_Q), f32)]
            + [pltpu.VMEM((n_pat, seq, LANES), f32)] * 3
            + [pltpu.VMEM((seq, LANES), f32)]
        ),
        compiler_params=pltpu.CompilerParams(
            dimension_semantics=("arbitrary", "arbitrary"), vmem_limit_bytes=VMEM_LIMIT_BYTES),
        name="attention",
    )(qkv, qkv, qkv)


def _gelu(x):
    return 0.5 * x * (1.0 + lax.erf(x * np.float32(1.0 / np.sqrt(2.0))))


def _sgu_pair_weights(w_ref):
    row = lax.broadcasted_iota(jnp.int32, (SGU_CHUNK, SGU_CHUNK), 0)
    col = lax.broadcasted_iota(jnp.int32, (SGU_CHUNK, SGU_CHUNK), 1)
    weights = [jnp.where(col <= row, w_ref[g], 0.0).astype(jnp.bfloat16) for g in range(SGU_GROUPS)]
    return [jnp.concatenate(weights[2 * sl:2 * sl + 2], axis=1) for sl in range(SGU_WIDTH // LANES)]


def _sgu_rows(u, v, gain, bias, pair_weights, b_mat):
    f32 = jnp.float32
    u = _gelu(u.astype(f32))
    v = _gelu(v.astype(f32))
    mu = jnp.mean(v, axis=-1, keepdims=True)
    vc = v - mu
    v = vc * lax.rsqrt(jnp.mean(vc * vc, axis=-1, keepdims=True) + LN_EPS)
    v = (v * gain + bias).astype(jnp.bfloat16)

    lane = lax.broadcasted_iota(jnp.int32, (SGU_CHUNK, LANES), 1)
    first_group = lane < (SGU_WIDTH // SGU_GROUPS)
    chunks = []
    for n in range(u.shape[0] // SGU_CHUNK):
        rows = slice(n * SGU_CHUNK, (n + 1) * SGU_CHUNK)
        slabs = []
        for sl in range(SGU_WIDTH // LANES):
            cols = slice(sl * LANES, (sl + 1) * LANES)
            vs = v[rows, cols]
            zero = jnp.zeros_like(vs)
            v_diag = jnp.concatenate([jnp.where(first_group, vs, zero), jnp.where(first_group, zero, vs)], axis=0)
            mixed = jnp.dot(pair_weights[sl], v_diag, preferred_element_type=f32) + b_mat[:, cols]
            slabs.append(u[rows, cols] * mixed)
        chunks.append(jnp.concatenate(slabs, axis=1))
    return jnp.concatenate(chunks, axis=0)


def _out_ffn_kernel(x_ref, attn_ref, sgu_ref, ga_ref, gs_ref, wo_ref, gpm_ref, gpf_ref,
                    wg_ref, wu_ref, wd_ref, gpo_ref, o_ref):
    f32 = jnp.float32
    bf16 = jnp.bfloat16
    sub = x_ref.shape[0] // FFN_SUBTILES
    row_slices = [slice(t * sub, (t + 1) * sub) for t in range(FFN_SUBTILES)]

    def out_proj(rows):
        a = _rms(attn_ref[rows, :].astype(f32), ga_ref[...]).astype(bf16)
        s = _rms(sgu_ref[rows, :].astype(f32), gs_ref[...]).astype(bf16)
        return (jnp.dot(a, wo_ref[0:ATTN_WIDTH, :], preferred_element_type=f32)
                + jnp.dot(s, wo_ref[ATTN_WIDTH:, :], preferred_element_type=f32))

    def ffn(rows, y):
        x1 = x_ref[rows, :] + _rms(y, gpm_ref[...])
        h = _rms(x1, gpf_ref[...]).astype(bf16)
        gate = jnp.dot(h, wg_ref[...], preferred_element_type=f32)
        up = jnp.dot(h, wu_ref[...], preferred_element_type=f32)
        act = (gate * jax.nn.sigmoid(gate) * up).astype(bf16)
        return x1, jnp.dot(act, wd_ref[...], preferred_element_type=f32)

    y_next = out_proj(row_slices[0])
    for t, rows in enumerate(row_slices):
        y = y_next
        if t + 1 < FFN_SUBTILES:
            y_next = out_proj(row_slices[t + 1])
        x1, f = ffn(rows, y)
        o_ref[rows, :] = x1 + _rms(f, gpo_ref[...])


def _out_ffn(x2, attn, sgu, ga, gs, wo, gpm, gpf, wg, wu, wd, gpo):
    n_tok = x2.shape[0]
    tm = FFN_TOKEN_TILE

    def const(shape):
        return pl.BlockSpec(shape, lambda i: (0,) * len(shape), pipeline_mode=pl.Buffered(1))

    return pl.pallas_call(
        _out_ffn_kernel,
        grid=(n_tok // tm,),
        in_specs=[
            pl.BlockSpec((tm, D_MODEL), lambda i: (i, 0)),
            pl.BlockSpec((tm, ATTN_WIDTH), lambda i: (i, 0)),
            pl.BlockSpec((tm, SGU_WIDTH), lambda i: (i, 0)),
            const((1, ATTN_WIDTH)),
            const((1, SGU_WIDTH)),
            const((D_MODEL, D_MODEL)),
            const((1, D_MODEL)),
            const((1, D_MODEL)),
            const((D_MODEL, D_FF)),
            const((D_MODEL, D_FF)),
            const((D_FF, D_MODEL)),
            const((1, D_MODEL)),
        ],
        out_specs=pl.BlockSpec((tm, D_MODEL), lambda i: (i, 0)),
        out_shape=jax.ShapeDtypeStruct((n_tok, D_MODEL), jnp.float32),
        compiler_params=pltpu.CompilerParams(
            dimension_semantics=("arbitrary",), vmem_limit_bytes=VMEM_LIMIT_BYTES),
        name="out_ffn",
    )(x2, attn, sgu, ga, gs, wo, gpm, gpf, wg, wu, wd, gpo)


def kernel(x, positions, pre_mix_norm, w_in, sgu_ln_gain, sgu_ln_bias, sgu_w_spatial, sgu_b_spatial, attn_out_norm, sgu_out_norm, w_out, post_mix_norm, pre_ffn_norm, w_gate, w_up, w_down, post_ffn_norm):
    batch, seq, _ = x.shape
    n_tok = batch * seq
    x2 = x.reshape(n_tok, D_MODEL)
    pos_row = positions.reshape(1, n_tok)

    depth = w_in.shape[0]
    for l in range(depth):
        b_mat = jnp.repeat(sgu_b_spatial[l].T, SGU_WIDTH // SGU_GROUPS, axis=1)
        qkv, sgu, (wo, wg, wu, wd) = _in_proj(
            x2, pos_row, pre_mix_norm[l][None, :], w_in[l],
            sgu_ln_gain[l][None, :], sgu_ln_bias[l][None, :], sgu_w_spatial[l], b_mat,
            later_weights=(w_out[l], w_gate[l], w_up[l], w_down[l]))
        attn = _attention(qkv, batch, seq)
        x2 = _out_ffn(x2, attn, sgu, attn_out_norm[l][None, :], sgu_out_norm[l][None, :],
                      wo, post_mix_norm[l][None, :], pre_ffn_norm[l][None, :], wg, wu, wd,
                      post_ffn_norm[l][None, :])
    return x2.reshape(batch, seq, D_MODEL)
```
